```python
import jax, jax.numpy as jnp
from jax import lax
import numpy as np

D_MODEL = 1024
BATCH = 16
SEQ = 4096
DEPTH = 4

MIX_WIDTH = D_MODEL
HGRN_WIDTH = MIX_WIDTH // 2
HGRN_HEADS = 4
HGRN_HEAD_DIM = HGRN_WIDTH // HGRN_HEADS
GLA_VAL = MIX_WIDTH - HGRN_WIDTH
GLA_HEADS = 4
GLA_KEY = GLA_VAL // 2
GLA_DK = GLA_KEY // GLA_HEADS
GLA_DV = GLA_VAL // GLA_HEADS
GLA_RANK = 16
GLA_GATE_NORM = 16.0
IN_SIZES = (HGRN_WIDTH, HGRN_WIDTH, HGRN_WIDTH, HGRN_WIDTH,
            GLA_KEY, GLA_KEY, GLA_VAL, GLA_VAL, GLA_RANK)
IN_DIM = sum(IN_SIZES)
CHUNK = 64
D_FF = 2816
CONV_W = 3
N_MOD = 6
EPS = 1e-6

kernel_name = "hybrid_hgrn2_gla_convffn_adaln"


def rms_norm(x, g):
    xf = x.astype(jnp.float32)
    y = xf * lax.rsqrt(jnp.mean(xf * xf, axis=-1, keepdims=True) + EPS)
    return (y * g.astype(jnp.float32)).astype(x.dtype)


def chunk_gated_linear_attention(q, k, v, log_a):
    B, S, H, DK = q.shape
    DV = v.shape[-1]
    N = S // CHUNK

    def to_chunks(t):
        return t.astype(jnp.float32).reshape(B, N, CHUNK, H, t.shape[-1]).transpose(0, 3, 1, 2, 4)

    q, k, v, g = to_chunks(q), to_chunks(k), to_chunks(v), to_chunks(log_a)
    b = jnp.cumsum(g, axis=3)
    b_last = b[:, :, :, -1:, :]
    ref = b[:, :, :, CHUNK // 2:CHUNK // 2 + 1, :]
    scores = jnp.einsum('bhncd,bhnsd->bhncs', q * jnp.exp(b - ref), k * jnp.exp(ref - b))
    causal = jnp.tril(jnp.ones((CHUNK, CHUNK), dtype=bool))
    scores = jnp.where(causal, scores, 0.0)
    o = jnp.einsum('bhncs,bhnse->bhnce', scores, v)
    u = jnp.einsum('bhncd,bhnce->nbhde', k * jnp.exp(b_last - b), v)
    decay = jnp.exp(b_last[:, :, :, 0, :]).transpose(2, 0, 1, 3)

    def step(state, inp):
        a_n, u_n = inp
        return a_n[..., None] * state + u_n, state

    s0 = jnp.zeros((B, H, DK, DV), jnp.float32)
    _, s_prev = lax.scan(step, s0, (decay, u))
    o = o + jnp.einsum('bhncd,nbhde->bhnce', q * jnp.exp(b), s_prev)
    return o.transpose(0, 2, 3, 1, 4).reshape(B, S, H, DV)


def gated_head_norm(o, gate, g):
    B, S = o.shape[:2]
    o = rms_norm(o, g).reshape(B, S, -1)
    return o * jax.nn.silu(gate.astype(jnp.float32))


def causal_dwconv(u, w, bias):
    S = u.shape[1]
    up = jnp.pad(u, ((0, 0), (CONV_W - 1, 0), (0, 0)))
    y = bias
    for j in range(CONV_W):
        y = y + up[:, j:j + S, :] * w[j]
    return y


def setup_inputs(seed: int = 0) -> dict:
    key = jax.random.key(seed)
    ks = jax.random.split(key, 20)
    f32 = jnp.float32
    nrm = lambda k, shape, s: jax.random.normal(k, shape, f32) * s
    L, D = DEPTH, D_MODEL
    return {
        "x": nrm(ks[0], (BATCH, SEQ, D), 1.0),
        "c": nrm(ks[1], (BATCH, D), 1.0),
        "ln1_g": 1.0 + nrm(ks[2], (L, D), 0.02),
        "ln2_g": 1.0 + nrm(ks[3], (L, D), 0.02),
        "w_ada": nrm(ks[4], (L, D, N_MOD * D), 0.5 * D ** -0.5),
        "b_ada": nrm(ks[5], (L, N_MOD * D), 0.01),
        "w_in": nrm(ks[6], (L, D, IN_DIM), D ** -0.5),
        "lb_params": nrm(ks[7], (L, HGRN_WIDTH), 0.1),
        "w_gk": nrm(ks[8], (L, GLA_RANK, GLA_KEY), GLA_RANK ** -0.5),
        "b_gk": nrm(ks[9], (L, GLA_KEY), 0.01),
        "gn_a": 1.0 + nrm(ks[10], (L, HGRN_HEAD_DIM), 0.02),
        "gn_b": 1.0 + nrm(ks[11], (L, GLA_DV), 0.02),
        "w_out": nrm(ks[12], (L, MIX_WIDTH, D), MIX_WIDTH ** -0.5),
        "w_up": nrm(ks[13], (L, D, 2 * D_FF), D ** -0.5),
        "conv_w": nrm(ks[14], (L, CONV_W, 2 * D_FF), CONV_W ** -0.5),
        "conv_b": nrm(ks[15], (L, 2 * D_FF), 0.01),
        "w_down": nrm(ks[16], (L, D_FF, D), D_FF ** -0.5),
        "lnf_g": 1.0 + nrm(ks[17], (D,), 0.02),
    }


def reference(x, c, ln1_g, ln2_g, w_ada, b_ada, w_in, lb_params, w_gk, b_gk,
              gn_a, gn_b, w_out, w_up, conv_w, conv_b, w_down, lnf_g):
    B, S, _ = x.shape
    split_idx = np.cumsum(IN_SIZES)[:-1].tolist()
    sm = jax.nn.softmax(lb_params.astype(jnp.float32), axis=0)
    lower_bounds = jnp.cumsum(sm, axis=0) - sm[0]
    cond = jax.nn.silu(c)
    for l in range(DEPTH):
        mod = jnp.einsum('bd,de->be', cond, w_ada[l]) + b_ada[l]
        sh1, sc1, gt1, sh2, sc2, gt2 = jnp.split(mod[:, None, :], N_MOD, axis=-1)

        h = rms_norm(x, ln1_g[l]) * (1.0 + sc1) + sh1
        p = jnp.einsum('bsd,de->bse', h, w_in[l])
        qa, fa, ia, ga, qb, kb, vb, gb, rb = jnp.split(p, split_idx, axis=-1)

        fa = fa.astype(jnp.float32)
        lb = lower_bounds[l]
        log_f = jnp.log(lb + (1.0 - lb) * jax.nn.sigmoid(fa))
        k_a = (1.0 - lb) * jax.nn.sigmoid(-fa)
        hs = (B, S, HGRN_HEADS, HGRN_HEAD_DIM)
        o_a = chunk_gated_linear_attention(qa.reshape(hs), k_a.reshape(hs), ia.reshape(hs), log_f.reshape(hs))
        o_a = gated_head_norm(o_a, ga, gn_a[l])

        gk = jnp.einsum('bsr,rk->bsk', rb.astype(jnp.float32), w_gk[l].astype(jnp.float32)) + b_gk[l]
        log_alpha = jax.nn.log_sigmoid(gk) / GLA_GATE_NORM
        ks_ = (B, S, GLA_HEADS, GLA_DK)
        q_b = qb.astype(jnp.float32) * GLA_DK ** -0.5
        o_b = chunk_gated_linear_attention(q_b.reshape(ks_), kb.reshape(ks_),
                                           vb.reshape(B, S, GLA_HEADS, GLA_DV), log_alpha.reshape(ks_))
        o_b = gated_head_norm(o_b, gb, gn_b[l])

        o = jnp.concatenate([o_a, o_b], axis=-1).astype(x.dtype)
        x = x + gt1 * jnp.einsum('bse,ed->bsd', o, w_out[l])

        h = rms_norm(x, ln2_g[l]) * (1.0 + sc2) + sh2
        u = jnp.einsum('bsd,df->bsf', h, w_up[l])
        u = causal_dwconv(u, conv_w[l], conv_b[l])
        a, v = jnp.split(u, 2, axis=-1)
        x = x + gt2 * jnp.einsum('bsf,fd->bsd', jax.nn.silu(a) * v, w_down[l])
    return rms_norm(x, lnf_g)
```

```python
import functools

import jax
import jax.numpy as jnp
from jax import lax
from jax.experimental import pallas as pl
from jax.experimental.pallas import tpu as pltpu

F32 = jnp.float32
BF16 = jnp.bfloat16

LANES = 128
SUBLANES = 8
VMEM_LIMIT_BYTES = 56 * 1024 * 1024

HGRN_HEADS = 4
GLA_HEADS = 4
GLA_RANK = 16
GLA_GATE_NORM = 16.0
CHUNK = 64
CONV_W = 3
N_MOD = 6
EPS = 1e-6

NT_DIMS = (((1,), (1,)), ((), ()))
TN_DIMS = (((0,), (0,)), ((), ()))


def _dot(a, b):
    return jnp.dot(a, b, preferred_element_type=F32)


def _dot_nt(a, b):
    return lax.dot_general(a, b, NT_DIMS, preferred_element_type=F32)


def _dot_tn(a, b):
    return lax.dot_general(a, b, TN_DIMS, preferred_element_type=F32)


def _split3(x):
    hi = x.astype(BF16)
    r = x - hi.astype(F32)
    mid = r.astype(BF16)
    lo = (r - mid.astype(F32)).astype(BF16)
    return hi, mid, lo


def _rms_rows(x):
    return x * lax.rsqrt(jnp.mean(x * x, axis=-1, keepdims=True) + EPS)


def _sigmoid(x):
    return 1.0 / (1.0 + jnp.exp(-x))


def _adaln_kernel(c_ref, w_ref, b_ref, o_ref):
    c = c_ref[...]
    cond = c * jax.nn.sigmoid(c)
    c_hi = cond.astype(BF16)
    c_lo = (cond - c_hi.astype(F32)).astype(BF16)
    w = w_ref[0]
    w_hi = w.astype(BF16)
    w_lo = (w - w_hi.astype(F32)).astype(BF16)
    acc = _dot(c_hi, w_lo) + _dot(c_lo, w_hi)
    o_ref[0] = (acc + _dot(c_hi, w_hi)) + b_ref[0]


def _adaln(c, w_ada, b_ada, n_block):
    depth, d, n = w_ada.shape
    b = c.shape[0]
    return pl.pallas_call(
        _adaln_kernel,
        grid=(depth, n // n_block),
        in_specs=[
            pl.BlockSpec((b, d), lambda l, j: (0, 0)),
            pl.BlockSpec((1, d, n_block), lambda l, j: (l, 0, j)),
            pl.BlockSpec((1, 1, n_block), lambda l, j: (l, 0, j)),
        ],
        out_specs=pl.BlockSpec((1, b, n_block), lambda l, j: (l, 0, j)),
        out_shape=jax.ShapeDtypeStruct((depth, b, n), F32),
        compiler_params=pltpu.CompilerParams(
            dimension_semantics=("arbitrary", "arbitrary"), vmem_limit_bytes=VMEM_LIMIT_BYTES),
        name="adaln",
    )(c, w_ada, b_ada.reshape(depth, 1, n))


def _cumsum_rows(tri, g):
    hi, mid, lo = _split3(g)
    return (_dot(tri, lo) + _dot(tri, mid)) + _dot(tri, hi)


def _decay_factors(b):
    half = CHUNK // 2
    ref = b[half:half + 1, :]
    last = b[CHUNK - 1:CHUNK, :]
    return jnp.exp(b - ref), jnp.exp(ref - b), jnp.exp(last - b), jnp.exp(b), jnp.exp(last)


def _head_norm_gate(o, gn, gate):
    return (_rms_rows(o) * gn) * (gate * jax.nn.sigmoid(gate))


def _mixer_kernel(layer, ts, x_ref, mod_ref, g1_ref, win_ref, lbp_ref, wgk_ref, bgk_ref, gna_ref, gnb_ref,
                  wout_ref, xo_ref, p_ref, o_ref, sa_ref, sb_ref):
    d = x_ref.shape[-1]
    wa = d // 2
    dka = wa // HGRN_HEADS
    kb_w = (d - wa) // 2
    dvb = (d - wa) // GLA_HEADS
    n_main = 4 * wa + 2 * kb_w + 2 * (d - wa)
    np_total = win_ref.shape[-1]

    @pl.when(pl.program_id(1) == 0)
    def _():
        sa_ref[...] = jnp.zeros_like(sa_ref)
        sb_ref[...] = jnp.zeros_like(sb_ref)

    sh1 = mod_ref[0, 0:1, :]
    sc1 = mod_ref[0, 1:2, :]
    gt1 = mod_ref[0, 2:3, :]

    x = x_ref[0]
    h = (_rms_rows(x) * g1_ref[...]) * (1.0 + sc1) + sh1
    hb = h.astype(BF16)
    col_block = 4 * LANES
    for c0 in range(0, np_total, col_block):
        c1 = min(c0 + col_block, np_total)
        p_ref[:, c0:c1] = _dot(hb, win_ref[:, c0:c1])

    rows = [lbp_ref[i:i + 1, :] for i in range(lbp_ref.shape[0])]
    mx = functools.reduce(jnp.maximum, rows)
    ex = [jnp.exp(r - mx) for r in rows]
    tot = functools.reduce(lambda a, b: a + b, ex)
    sm = [e / tot for e in ex]
    cs = sm[0]
    for i in range(1, layer + 1):
        cs = cs + sm[i]
    lb = cs - sm[0]
    one_m_lb = 1.0 - lb

    ri = lax.broadcasted_iota(jnp.int32, (CHUNK, CHUNK), 0)
    ci = lax.broadcasted_iota(jnp.int32, (CHUNK, CHUNK), 1)
    causal = ri >= ci
    tri = jnp.where(causal, 1.0, 0.0).astype(BF16)
    lane = lax.broadcasted_iota(jnp.int32, (1, LANES), 1)
    half_masks = (lane < LANES // 2, lane >= LANES // 2)

    gna = gna_ref[...]
    gnb = gnb_ref[...]
    bgk = bgk_ref[...]

    def chunk_body(c, carry):
        r0 = pl.multiple_of(c * CHUNK, CHUNK)
        rows_c = pl.ds(r0, CHUNK)

        qa = p_ref[rows_c, 0:wa]
        fa = p_ref[rows_c, wa:2 * wa]
        ia = p_ref[rows_c, 2 * wa:3 * wa]
        ga = p_ref[rows_c, 3 * wa:4 * wa]
        log_f = jnp.log(lb + one_m_lb * jax.nn.sigmoid(fa))
        k_a = one_m_lb * jax.nn.sigmoid(-fa)
        b_a = _cumsum_rows(tri, log_f)
        e_q, e_k, e_u, e_s, e_last = _decay_factors(b_a)
        qe = (qa * e_q).astype(BF16)
        ke = (k_a * e_k).astype(BF16)
        kd = (k_a * e_u).astype(BF16)
        qs = (qa * e_s).astype(BF16)
        vb_a = ia.astype(BF16)
        for hd in range(HGRN_HEADS):
            sl = slice(hd * dka, (hd + 1) * dka)
            scores = jnp.where(causal, _dot_nt(qe[:, sl], ke[:, sl]), 0.0).astype(BF16)
            st = sa_ref[hd]
            o = _dot(scores, vb_a[:, sl]) + _dot_nt(qs[:, sl], st.astype(BF16))
            sa_ref[hd] = st * e_last[:, sl] + _dot_tn(vb_a[:, sl], kd[:, sl])
            o_ref[rows_c, sl] = _head_norm_gate(o, gna, ga[:, sl]).astype(BF16)

        base = 4 * wa
        qb = p_ref[rows_c, base:base + kb_w] * (float(kb_w // GLA_HEADS) ** -0.5)
        kb = p_ref[rows_c, base + kb_w:base + 2 * kb_w]
        vbb = p_ref[rows_c, base + 2 * kb_w:base + 2 * kb_w + (d - wa)].astype(BF16)
        gb = p_ref[rows_c, base + 2 * kb_w + (d - wa):n_main]
        rb = p_ref[rows_c, n_main:np_total].astype(BF16)
        gk = _dot(rb, wgk_ref[...]) + bgk
        log_alpha = (jnp.minimum(gk, 0.0) - jnp.log1p(jnp.exp(-jnp.abs(gk)))) / GLA_GATE_NORM
        b_b = _cumsum_rows(tri, log_alpha)
        e_q, e_k, e_u, e_s, e_last = _decay_factors(b_b)
        qe = (qb * e_q).astype(BF16)
        ke = kb * e_k
        kd = kb * e_u
        qs = (qb * e_s).astype(BF16)
        for hd in range(GLA_HEADS):
            pair = slice((hd // 2) * LANES, (hd // 2 + 1) * LANES)
            msk = half_masks[hd % 2]
            vs = slice(hd * dvb, (hd + 1) * dvb)
            ke_h = jnp.where(msk, ke[:, pair], 0.0).astype(BF16)
            kd_h = jnp.where(msk, kd[:, pair], 0.0).astype(BF16)
            scores = jnp.where(causal, _dot_nt(qe[:, pair], ke_h), 0.0).astype(BF16)
            st = sb_ref[hd]
            o = _dot(scores, vbb[:, vs]) + _dot_nt(qs[:, pair], st.astype(BF16))
            sb_ref[hd] = st * e_last[:, pair] + _dot_tn(vbb[:, vs], kd_h)
            o_ref[rows_c, wa + hd * dvb:wa + (hd + 1) * dvb] = _head_norm_gate(o, gnb, gb[:, vs]).astype(BF16)
        return carry

    lax.fori_loop(0, ts // CHUNK, chunk_body, 0)

    ob = o_ref[...]
    for c0 in range(0, d, col_block):
        c1 = c0 + col_block
        xo_ref[0, :, c0:c1] = x_ref[0, :, c0:c1] + gt1[:, c0:c1] * _dot(ob, wout_ref[:, c0:c1])


def _mixer(x, mod, g1, w_in_p, lb_params, w_gk_p, b_gk, gn_a, gn_b, w_out_b, layer, ts):
    bsz, seq, d = x.shape
    np_total = w_in_p.shape[-1]
    const = lambda b, s: (0, 0)
    return pl.pallas_call(
        functools.partial(_mixer_kernel, layer, ts),
        grid=(bsz, seq // ts),
        in_specs=[
            pl.BlockSpec((1, ts, d), lambda b, s: (b, s, 0)),
            pl.BlockSpec((1, N_MOD, d), lambda b, s: (b, 0, 0)),
            pl.BlockSpec(g1.shape, const),
            pl.BlockSpec(w_in_p.shape, const),
            pl.BlockSpec(lb_params.shape, const),
            pl.BlockSpec(w_gk_p.shape, const),
            pl.BlockSpec(b_gk.shape, const),
            pl.BlockSpec(gn_a.shape, const),
            pl.BlockSpec(gn_b.shape, const),
            pl.BlockSpec(w_out_b.shape, const),
        ],
        out_specs=pl.BlockSpec((1, ts, d), lambda b, s: (b, s, 0)),
        out_shape=jax.ShapeDtypeStruct(x.shape, F32),
        scratch_shapes=[
            pltpu.VMEM((ts, np_total), F32),
            pltpu.VMEM((ts, d), BF16),
            pltpu.VMEM((HGRN_HEADS, LANES, LANES), F32),
            pltpu.VMEM((GLA_HEADS, LANES, LANES), F32),
        ],
        compiler_params=pltpu.CompilerParams(
            dimension_semantics=("arbitrary", "arbitrary"), vmem_limit_bytes=VMEM_LIMIT_BYTES),
        name="mixer",
    )(x, mod, g1, w_in_p, lb_params, w_gk_p, b_gk, gn_a, gn_b, w_out_b)


def _ffn_kernel(final, ts, fblk, x_ref, mod_ref, g2_ref, wup_ref, cw_ref, cb_ref, wdn_ref, gf_ref,
                xo_ref, hb_ref, ub_ref, halo_ref, act_ref):
    d = x_ref.shape[-1]
    f = wdn_ref.shape[0]
    halo = SUBLANES

    @pl.when(pl.program_id(1) == 0)
    def _():
        halo_ref[...] = jnp.zeros_like(halo_ref)

    sh2 = mod_ref[0, 3:4, :]
    sc2 = mod_ref[0, 4:5, :]
    gt2 = mod_ref[0, 5:6, :]

    x = x_ref[0]
    hb_ref[...] = ((_rms_rows(x) * g2_ref[...]) * (1.0 + sc2) + sh2).astype(BF16)

    for j in range(f // fblk):
        cols = slice(2 * fblk * j, 2 * fblk * (j + 1))
        u = _dot(hb_ref[...], wup_ref[:, cols])
        ub_ref[0:halo, :] = halo_ref[:, cols]
        ub_ref[halo:halo + ts, :] = u
        halo_ref[:, cols] = u[ts - halo:ts, :]
        y = cb_ref[:, cols]
        y = y + ub_ref[halo - 2:halo - 2 + ts, :] * cw_ref[0:1, cols]
        y = y + ub_ref[halo - 1:halo - 1 + ts, :] * cw_ref[1:2, cols]
        y = y + u * cw_ref[2:3, cols]
        a = y[:, :fblk]
        v = y[:, fblk:]
        act_ref[:, fblk * j:fblk * (j + 1)] = ((a * jax.nn.sigmoid(a)) * v).astype(BF16)

    act = act_ref[...]
    col_block = 4 * LANES
    for c0 in range(0, d, col_block):
        c1 = c0 + col_block
        xo_ref[0, :, c0:c1] = x_ref[0, :, c0:c1] + gt2[:, c0:c1] * _dot(act, wdn_ref[:, c0:c1])
    if final:
        xo_ref[0] = _rms_rows(xo_ref[0]) * gf_ref[...]


def _ffn(x, mod, g2, w_up_p, conv_w_p, conv_b_p, w_down_b, lnf_g, final, ts, fblk):
    bsz, seq, d = x.shape
    f2 = w_up_p.shape[-1]
    const = lambda b, s: (0, 0)
    return pl.pallas_call(
        functools.partial(_ffn_kernel, final, ts, fblk),
        grid=(bsz, seq // ts),
        in_specs=[
            pl.BlockSpec((1, ts, d), lambda b, s: (b, s, 0)),
            pl.BlockSpec((1, N_MOD, d), lambda b, s: (b, 0, 0)),
            pl.BlockSpec(g2.shape, const),
            pl.BlockSpec(w_up_p.shape, const),
            pl.BlockSpec(conv_w_p.shape, const),
            pl.BlockSpec(conv_b_p.shape, const),
            pl.BlockSpec(w_down_b.shape, const),
            pl.BlockSpec(lnf_g.shape, const),
        ],
        out_specs=pl.BlockSpec((1, ts, d), lambda b, s: (b, s, 0)),
        out_shape=jax.ShapeDtypeStruct(x.shape, F32),
        scratch_shapes=[
            pltpu.VMEM((ts, d), BF16),
            pltpu.VMEM((SUBLANES + ts, 2 * fblk), F32),
            pltpu.VMEM((SUBLANES, f2), F32),
            pltpu.VMEM((ts, f2 // 2), BF16),
        ],
        compiler_params=pltpu.CompilerParams(
            dimension_semantics=("arbitrary", "arbitrary"), vmem_limit_bytes=VMEM_LIMIT_BYTES),
        name="ffn_final" if final else "ffn",
    )(x, mod, g2, w_up_p, conv_w_p, conv_b_p, w_down_b, lnf_g)


def _interleave_halves(w, fblk):
    f = w.shape[-1] // 2
    a = w[..., :f].reshape(w.shape[:-1] + (f // fblk, 1, fblk))
    v = w[..., f:].reshape(w.shape[:-1] + (f // fblk, 1, fblk))
    return jnp.concatenate([a, v], axis=-2).reshape(w.shape)


def kernel(x, c, ln1_g, ln2_g, w_ada, b_ada, w_in, lb_params, w_gk, b_gk, gn_a, gn_b, w_out, w_up, conv_w,
           conv_b, w_down, lnf_g):
    bsz, seq, d = x.shape
    depth = w_ada.shape[0]
    ts = min(512, seq)
    fblk = 2 * LANES
    assert seq % ts == 0 and ts % CHUNK == 0 and d % (4 * LANES) == 0
    assert (w_up.shape[-1] // 2) % fblk == 0

    mod_all = _adaln(c, w_ada, b_ada, n_block=N_MOD * d // 4).reshape(depth, bsz, N_MOD, d)

    n_main = w_in.shape[-1] - GLA_RANK
    rank_pad = LANES - GLA_RANK
    w_in_p = jnp.pad(w_in, ((0, 0), (0, 0), (0, rank_pad))).astype(BF16)
    w_gk_p = jnp.pad(w_gk, ((0, 0), (0, rank_pad), (0, 0))).astype(BF16)
    w_out_b = w_out.astype(BF16)
    w_up_p = _interleave_halves(w_up, fblk).astype(BF16)
    conv_w_p = _interleave_halves(conv_w, fblk)
    conv_b_p = _interleave_halves(conv_b, fblk)
    w_down_b = w_down.astype(BF16)
    del n_main

    for l in range(depth):
        x = _mixer(x, mod_all[l], ln1_g[l][None, :], w_in_p[l], lb_params, w_gk_p[l], b_gk[l][None, :],
                   gn_a[l][None, :], gn_b[l][None, :], w_out_b[l], l, ts)
        x = _ffn(x, mod_all[l], ln2_g[l][None, :], w_up_p[l], conv_w_p[l], conv_b_p[l][None, :], w_down_b[l],
                 lnf_g[None, :], l == depth - 1, ts, fblk)
    return x
```

```python
import functools

import jax
import jax.numpy as jnp
from jax import lax
from jax.experimental import pallas as pl
from jax.experimental.pallas import tpu as pltpu

F32 = jnp.float32
BF16 = jnp.bfloat16

LANES = 128
SUBLANES = 8
VMEM_LIMIT_BYTES = 56 * 1024 * 1024

HGRN_HEADS = 4
GLA_HEADS = 4
GLA_RANK = 16
GLA_GATE_NORM = 16.0
CHUNK = 64
CONV_W = 3
N_MOD = 6
EPS = 1e-6

NT_DIMS = (((1,), (1,)), ((), ()))
TN_DIMS = (((0,), (0,)), ((), ()))


def _dot(a, b):
    return jnp.dot(a, b, preferred_element_type=F32)


def _dot_nt(a, b):
    return lax.dot_general(a, b, NT_DIMS, preferred_element_type=F32)


def _dot_tn(a, b):
    return lax.dot_general(a, b, TN_DIMS, preferred_element_type=F32)


def _split3(x):
    hi = x.astype(BF16)
    r = x - hi.astype(F32)
    mid = r.astype(BF16)
    lo = (r - mid.astype(F32)).astype(BF16)
    return hi, mid, lo


def _rms_rows(x):
    return x * lax.rsqrt(jnp.mean(x * x, axis=-1, keepdims=True) + EPS)


def _adaln_kernel(c_ref, w_ref, b_ref, o_ref):
    c = c_ref[...]
    cond = c * jax.nn.sigmoid(c)
    c_hi = cond.astype(BF16)
    c_lo = (cond - c_hi.astype(F32)).astype(BF16)
    w = w_ref[0]
    w_hi = w.astype(BF16)
    w_lo = (w - w_hi.astype(F32)).astype(BF16)
    acc = _dot(c_hi, w_lo) + _dot(c_lo, w_hi)
    o_ref[0] = (acc + _dot(c_hi, w_hi)) + b_ref[0]


def _adaln(c, w_ada, b_ada, n_block):
    depth, d, n = w_ada.shape
    b = c.shape[0]
    return pl.pallas_call(
        _adaln_kernel,
        grid=(depth, n // n_block),
        in_specs=[
            pl.BlockSpec((b, d), lambda l, j: (0, 0)),
            pl.BlockSpec((1, d, n_block), lambda l, j: (l, 0, j)),
            pl.BlockSpec((1, 1, n_block), lambda l, j: (l, 0, j)),
        ],
        out_specs=pl.BlockSpec((1, b, n_block), lambda l, j: (l, 0, j)),
        out_shape=jax.ShapeDtypeStruct((depth, b, n), F32),
        compiler_params=pltpu.CompilerParams(
            dimension_semantics=("arbitrary", "arbitrary"), vmem_limit_bytes=VMEM_LIMIT_BYTES),
        name="adaln",
    )(c, w_ada, b_ada.reshape(depth, 1, n))


def _cumsum_rows(tri, g):
    hi, mid, lo = _split3(g)
    return (_dot(tri, lo) + _dot(tri, mid)) + _dot(tri, hi)


def _decay_factors(b):
    half = CHUNK // 2
    ref = b[half:half + 1, :]
    last = b[CHUNK - 1:CHUNK, :]
    return jnp.exp(b - ref), jnp.exp(ref - b), jnp.exp(ref), jnp.exp(last - ref), jnp.exp(last)


def _head_norm_gate(o, gn, gate):
    return (_rms_rows(o) * gn) * (gate * jax.nn.sigmoid(gate))


def _mixer_kernel(layer, ts, x_ref, mod_ref, g1_ref, win_ref, lbp_ref, wgk_ref, bgk_ref, gna_ref, gnb_ref,
                  wout_ref, xo_ref, p_ref, o_ref, sa_ref, sb_ref, qk_ref, kb_ref, qb_ref, v_ref, el_ref, sc_ref):
    d = x_ref.shape[-1]
    wa = d // 2
    wb = d - wa
    dka = wa // HGRN_HEADS
    kb_w = wb // 2
    dvb = wb // GLA_HEADS
    n_main = 4 * wa + 2 * kb_w + 2 * wb
    np_total = win_ref.shape[-1]
    n_chunks = ts // CHUNK

    @pl.when(pl.program_id(1) == 0)
    def _():
        sa_ref[...] = jnp.zeros_like(sa_ref)
        sb_ref[...] = jnp.zeros_like(sb_ref)

    sh1 = mod_ref[0, 0:1, :]
    sc1 = mod_ref[0, 1:2, :]
    gt1 = mod_ref[0, 2:3, :]

    x = x_ref[0]
    h = (_rms_rows(x) * g1_ref[...]) * (1.0 + sc1) + sh1
    hb = h.astype(BF16)
    col_block = 4 * LANES
    for c0 in range(0, np_total, col_block):
        c1 = min(c0 + col_block, np_total)
        p_ref[:, c0:c1] = _dot(hb, win_ref[:, c0:c1])

    rows = [lbp_ref[i:i + 1, :] for i in range(lbp_ref.shape[0])]
    mx = functools.reduce(jnp.maximum, rows)
    ex = [jnp.exp(r - mx) for r in rows]
    tot = functools.reduce(lambda a, b: a + b, ex)
    sm = [e / tot for e in ex]
    cs = sm[0]
    for i in range(1, layer + 1):
        cs = cs + sm[i]
    lb = cs - sm[0]
    one_m_lb = 1.0 - lb

    ri = lax.broadcasted_iota(jnp.int32, (CHUNK, CHUNK), 0)
    ci = lax.broadcasted_iota(jnp.int32, (CHUNK, CHUNK), 1)
    causal = ri >= ci
    tri = jnp.where(causal, 1.0, 0.0).astype(BF16)
    lane = lax.broadcasted_iota(jnp.int32, (CHUNK, LANES), 1)
    half_masks = (lane < LANES // 2, lane >= LANES // 2)

    gna = gna_ref[...]
    gnb = gnb_ref[...]
    bgk = bgk_ref[...]
    base_b = 4 * wa


    def stage_factors(c):
        rows_c = slice(c * CHUNK, (c + 1) * CHUNK)
        qa = p_ref[rows_c, 0:wa]
        fa = p_ref[rows_c, wa:2 * wa]
        log_f = jnp.log(lb + one_m_lb * jax.nn.sigmoid(fa))
        k_a = one_m_lb * jax.nn.sigmoid(-fa)
        e_q, e_k, r_s, r_u, e_last = _decay_factors(_cumsum_rows(tri, log_f))
        qe = qa * e_q
        ke = k_a * e_k
        qk_ref[0, rows_c, :] = qe.astype(BF16)
        qk_ref[1, rows_c, :] = ke.astype(BF16)
        qk_ref[2, rows_c, :] = (qe * r_s).astype(BF16)
        qk_ref[3, rows_c, :] = (ke * r_u).astype(BF16)
        el_ref[c:c + 1, 0:wa] = e_last
        v_ref[rows_c, 0:wa] = p_ref[rows_c, 2 * wa:3 * wa].astype(BF16)
        qb = p_ref[rows_c, base_b:base_b + kb_w] * (float(kb_w // GLA_HEADS) ** -0.5)
        kb = p_ref[rows_c, base_b + kb_w:base_b + 2 * kb_w]
        rb = p_ref[rows_c, n_main:np_total].astype(BF16)
        gk = _dot(rb, wgk_ref[...]) + bgk
        log_alpha = (jnp.minimum(gk, 0.0) - jnp.log1p(jnp.exp(-jnp.abs(gk)))) / GLA_GATE_NORM
        e_q, e_k, r_s, r_u, e_last = _decay_factors(_cumsum_rows(tri, log_alpha))
        qe = qb * e_q
        ke = kb * e_k
        qb_ref[0, rows_c, :] = qe.astype(BF16)
        qb_ref[1, rows_c, :] = (qe * r_s).astype(BF16)
        ke_b = ke.astype(BF16)
        kd_b = (ke * r_u).astype(BF16)
        zero = jnp.zeros((CHUNK, LANES), BF16)
        for hd in range(GLA_HEADS):
            pair = slice((hd // 2) * LANES, (hd // 2 + 1) * LANES)
            msk = half_masks[hd % 2]
            kb_ref[0, rows_c, hd * LANES:(hd + 1) * LANES] = jnp.where(msk, ke_b[:, pair], zero)
            kb_ref[1, rows_c, hd * LANES:(hd + 1) * LANES] = jnp.where(msk, kd_b[:, pair], zero)
        el_ref[c:c + 1, wa:wa + kb_w] = e_last
        v_ref[rows_c, wa:d] = p_ref[rows_c, base_b + 2 * kb_w:base_b + 2 * kb_w + wb].astype(BF16)

    def stage_scores(c):
        rows_c = slice(c * CHUNK, (c + 1) * CHUNK)
        for hd in range(HGRN_HEADS):
            sl = slice(hd * dka, (hd + 1) * dka)
            scores = jnp.where(causal, _dot_nt(qk_ref[0, rows_c, sl], qk_ref[1, rows_c, sl]), 0.0)
            sc_ref[rows_c, hd * LANES:hd * LANES + CHUNK] = scores.astype(BF16)
        for hd in range(GLA_HEADS):
            pair = slice((hd // 2) * LANES, (hd // 2 + 1) * LANES)
            hs = slice(hd * LANES, (hd + 1) * LANES)
            scores = jnp.where(causal, _dot_nt(qb_ref[0, rows_c, pair], kb_ref[0, rows_c, hs]), 0.0)
            g = HGRN_HEADS + hd
            sc_ref[rows_c, g * LANES:g * LANES + CHUNK] = scores.astype(BF16)

    def stage_state(c):
        rows_c = slice(c * CHUNK, (c + 1) * CHUNK)
        for hd in range(HGRN_HEADS):
            sl = slice(hd * dka, (hd + 1) * dka)
            st = sa_ref[hd]
            sc = sc_ref[rows_c, hd * LANES:hd * LANES + CHUNK]
            o = _dot(sc, v_ref[rows_c, sl]) + _dot_nt(qk_ref[2, rows_c, sl], st.astype(BF16))
            sa_ref[hd] = st * el_ref[c:c + 1, sl] + _dot_tn(v_ref[rows_c, sl], qk_ref[3, rows_c, sl])
            gate = p_ref[rows_c, 3 * wa + hd * dka:3 * wa + (hd + 1) * dka]
            o_ref[rows_c, sl] = _head_norm_gate(o, gna, gate).astype(BF16)
        for hd in range(GLA_HEADS):
            pair = slice((hd // 2) * LANES, (hd // 2 + 1) * LANES)
            hs = slice(hd * LANES, (hd + 1) * LANES)
            vs = slice(wa + hd * dvb, wa + (hd + 1) * dvb)
            st = sb_ref[hd]
            g = HGRN_HEADS + hd
            sc = sc_ref[rows_c, g * LANES:g * LANES + CHUNK]
            o = _dot(sc, v_ref[rows_c, vs]) + _dot_nt(qb_ref[1, rows_c, pair], st.astype(BF16))
            el_pair = el_ref[c:c + 1, wa + (hd // 2) * LANES:wa + (hd // 2 + 1) * LANES]
            sb_ref[hd] = st * el_pair + _dot_tn(v_ref[rows_c, vs], kb_ref[1, rows_c, hs])
            g0 = base_b + 2 * kb_w + wb + hd * dvb
            o_ref[rows_c, vs] = _head_norm_gate(o, gnb, p_ref[rows_c, g0:g0 + dvb]).astype(BF16)

    for step in range(n_chunks + 2):
        if step >= 2:
            stage_state(step - 2)
        if 1 <= step <= n_chunks:
            stage_scores(step - 1)
        if step < n_chunks:
            stage_factors(step)

    ob = o_ref[...]
    for c0 in range(0, d, col_block):
        c1 = c0 + col_block
        xo_ref[0, :, c0:c1] = x_ref[0, :, c0:c1] + gt1[:, c0:c1] * _dot(ob, wout_ref[:, c0:c1])


def _mixer(x, mod, g1, w_in_p, lb_params, w_gk_p, b_gk, gn_a, gn_b, w_out_b, layer, ts):
    bsz, seq, d = x.shape
    np_total = w_in_p.shape[-1]
    wa = d // 2
    kb_w = (d - wa) // 2
    const = lambda b, s: (0, 0)
    return pl.pallas_call(
        functools.partial(_mixer_kernel, layer, ts),
        grid=(bsz, seq // ts),
        in_specs=[
            pl.BlockSpec((1, ts, d), lambda b, s: (b, s, 0)),
            pl.BlockSpec((1, N_MOD, d), lambda b, s: (b, 0, 0)),
            pl.BlockSpec(g1.shape, const),
            pl.BlockSpec(w_in_p.shape, const),
            pl.BlockSpec(lb_params.shape, const),
            pl.BlockSpec(w_gk_p.shape, const),
            pl.BlockSpec(b_gk.shape, const),
            pl.BlockSpec(gn_a.shape, const),
            pl.BlockSpec(gn_b.shape, const),
            pl.BlockSpec(w_out_b.shape, const),
        ],
        out_specs=pl.BlockSpec((1, ts, d), lambda b, s: (b, s, 0)),
        out_shape=jax.ShapeDtypeStruct(x.shape, F32),
        scratch_shapes=[
            pltpu.VMEM((ts, np_total), F32),
            pltpu.VMEM((ts, d), BF16),
            pltpu.VMEM((HGRN_HEADS, LANES, LANES), F32),
            pltpu.VMEM((GLA_HEADS, LANES, LANES), F32),
            pltpu.VMEM((4, ts, wa), BF16),
            pltpu.VMEM((2, ts, GLA_HEADS * LANES), BF16),
            pltpu.VMEM((2, ts, kb_w), BF16),
            pltpu.VMEM((ts, d), BF16),
            pltpu.VMEM((ts // CHUNK, wa + kb_w), F32),
            pltpu.VMEM((ts, (HGRN_HEADS + GLA_HEADS) * LANES), BF16),
        ],
        compiler_params=pltpu.CompilerParams(
            dimension_semantics=("arbitrary", "arbitrary"), vmem_limit_bytes=VMEM_LIMIT_BYTES),
        name="mixer",
    )(x, mod, g1, w_in_p, lb_params, w_gk_p, b_gk, gn_a, gn_b, w_out_b)


def _ffn_kernel(final, ts, fblk, x_ref, mod_ref, g2_ref, wup_ref, cw_ref, cb_ref, wdn_ref, gf_ref,
                xo_ref, hb_ref, ub_ref, halo_ref, act_ref):
    d = x_ref.shape[-1]
    f = wdn_ref.shape[0]
    halo = SUBLANES

    @pl.when(pl.program_id(1) == 0)
    def _():
        halo_ref[...] = jnp.zeros_like(halo_ref)

    sh2 = mod_ref[0, 3:4, :]
    sc2 = mod_ref[0, 4:5, :]
    gt2 = mod_ref[0, 5:6, :]

    x = x_ref[0]
    hb_ref[...] = ((_rms_rows(x) * g2_ref[...]) * (1.0 + sc2) + sh2).astype(BF16)

    for j in range(f // fblk):
        cols = slice(2 * fblk * j, 2 * fblk * (j + 1))
        u = _dot(hb_ref[...], wup_ref[:, cols])
        ub_ref[0:halo, :] = halo_ref[:, cols]
        ub_ref[halo:halo + ts, :] = u
        halo_ref[:, cols] = u[ts - halo:ts, :]
        y = cb_ref[:, cols]
        y = y + ub_ref[halo - 2:halo - 2 + ts, :] * cw_ref[0:1, cols]
        y = y + ub_ref[halo - 1:halo - 1 + ts, :] * cw_ref[1:2, cols]
        y = y + u * cw_ref[2:3, cols]
        a = y[:, :fblk]
        v = y[:, fblk:]
        act_ref[:, fblk * j:fblk * (j + 1)] = ((a * jax.nn.sigmoid(a)) * v).astype(BF16)

    act = act_ref[...]
    col_block = 4 * LANES
    for c0 in range(0, d, col_block):
        c1 = c0 + col_block
        xo_ref[0, :, c0:c1] = x_ref[0, :, c0:c1] + gt2[:, c0:c1] * _dot(act, wdn_ref[:, c0:c1])
    if final:
        xo_ref[0] = _rms_rows(xo_ref[0]) * gf_ref[...]


def _ffn(x, mod, g2, w_up_p, conv_w_p, conv_b_p, w_down_b, lnf_g, final, ts, fblk):
    bsz, seq, d = x.shape
    f2 = w_up_p.shape[-1]
    const = lambda b, s: (0, 0)
    return pl.pallas_call(
        functools.partial(_ffn_kernel, final, ts, fblk),
        grid=(bsz, seq // ts),
        in_specs=[
            pl.BlockSpec((1, ts, d), lambda b, s: (b, s, 0)),
            pl.BlockSpec((1, N_MOD, d), lambda b, s: (b, 0, 0)),
            pl.BlockSpec(g2.shape, const),
            pl.BlockSpec(w_up_p.shape, const),
            pl.BlockSpec(conv_w_p.shape, const),
            pl.BlockSpec(conv_b_p.shape, const),
            pl.BlockSpec(w_down_b.shape, const),
            pl.BlockSpec(lnf_g.shape, const),
        ],
        out_specs=pl.BlockSpec((1, ts, d), lambda b, s: (b, s, 0)),
        out_shape=jax.ShapeDtypeStruct(x.shape, F32),
        scratch_shapes=[
            pltpu.VMEM((ts, d), BF16),
            pltpu.VMEM((SUBLANES + ts, 2 * fblk), F32),
            pltpu.VMEM((SUBLANES, f2), F32),
            pltpu.VMEM((ts, f2 // 2), BF16),
        ],
        compiler_params=pltpu.CompilerParams(
            dimension_semantics=("arbitrary", "arbitrary"), vmem_limit_bytes=VMEM_LIMIT_BYTES),
        name="ffn_final" if final else "ffn",
    )(x, mod, g2, w_up_p, conv_w_p, conv_b_p, w_down_b, lnf_g)


def _interleave_halves(w, fblk):
    f = w.shape[-1] // 2
    a = w[..., :f].reshape(w.shape[:-1] + (f // fblk, 1, fblk))
    v = w[..., f:].reshape(w.shape[:-1] + (f // fblk, 1, fblk))
    return jnp.concatenate([a, v], axis=-2).reshape(w.shape)


def kernel(x, c, ln1_g, ln2_g, w_ada, b_ada, w_in, lb_params, w_gk, b_gk, gn_a, gn_b, w_out, w_up, conv_w,
           conv_b, w_down, lnf_g):
    bsz, seq, d = x.shape
    depth = w_ada.shape[0]
    ts = min(512, seq)
    fblk = 2 * LANES
    assert seq % ts == 0 and ts % CHUNK == 0 and d % (4 * LANES) == 0
    assert (w_up.shape[-1] // 2) % fblk == 0

    mod_all = _adaln(c, w_ada, b_ada, n_block=N_MOD * d // 4).reshape(depth, bsz, N_MOD, d)

    rank_pad = LANES - GLA_RANK
    w_in_p = jnp.pad(w_in, ((0, 0), (0, 0), (0, rank_pad))).astype(BF16)
    w_gk_p = jnp.pad(w_gk, ((0, 0), (0, rank_pad), (0, 0))).astype(BF16)
    w_out_b = w_out.astype(BF16)
    w_up_p = _interleave_halves(w_up, fblk).astype(BF16)
    conv_w_p = _interleave_halves(conv_w, fblk)
    conv_b_p = _interleave_halves(conv_b, fblk)
    w_down_b = w_down.astype(BF16)

    for l in range(depth):
        x = _mixer(x, mod_all[l], ln1_g[l][None, :], w_in_p[l], lb_params, w_gk_p[l], b_gk[l][None, :],
                   gn_a[l][None, :], gn_b[l][None, :], w_out_b[l], l, ts)
        x = _ffn(x, mod_all[l], ln2_g[l][None, :], w_up_p[l], conv_w_p[l], conv_b_p[l][None, :], w_down_b[l],
                 lnf_g[None, :], l == depth - 1, ts, fblk)
    return x
```

```python
import functools

import jax
import jax.numpy as jnp
from jax import lax
from jax.experimental import pallas as pl
from jax.experimental.pallas import tpu as pltpu

F32 = jnp.float32
BF16 = jnp.bfloat16

LANES = 128
SUBLANES = 8
VMEM_LIMIT_BYTES = 56 * 1024 * 1024

HGRN_HEADS = 4
GLA_HEADS = 4
GLA_RANK = 16
GLA_GATE_NORM = 16.0
CHUNK = 64
CONV_W = 3
N_MOD = 6
EPS = 1e-6

NT_DIMS = (((1,), (1,)), ((), ()))
TN_DIMS = (((0,), (0,)), ((), ()))


def _dot(a, b):
    return jnp.dot(a, b, preferred_element_type=F32)


def _dot_nt(a, b):
    return lax.dot_general(a, b, NT_DIMS, preferred_element_type=F32)


def _dot_tn(a, b):
    return lax.dot_general(a, b, TN_DIMS, preferred_element_type=F32)


def _split2(x):
    hi = x.astype(BF16)
    lo = (x - hi.astype(F32)).astype(BF16)
    return hi, lo


def _rms_rows(x):
    return x * lax.rsqrt(jnp.mean(x * x, axis=-1, keepdims=True) + EPS)


def _adaln_kernel(c_ref, w_ref, b_ref, o_ref):
    c = c_ref[...]
    cond = c * jax.nn.sigmoid(c)
    c_hi = cond.astype(BF16)
    c_lo = (cond - c_hi.astype(F32)).astype(BF16)
    w = w_ref[0]
    w_hi = w.astype(BF16)
    w_lo = (w - w_hi.astype(F32)).astype(BF16)
    acc = _dot(c_hi, w_lo) + _dot(c_lo, w_hi)
    o_ref[0] = (acc + _dot(c_hi, w_hi)) + b_ref[0]


def _adaln(c, w_ada, b_ada, n_block):
    depth, d, n = w_ada.shape
    b = c.shape[0]
    return pl.pallas_call(
        _adaln_kernel,
        grid=(depth, n // n_block),
        in_specs=[
            pl.BlockSpec((b, d), lambda l, j: (0, 0)),
            pl.BlockSpec((1, d, n_block), lambda l, j: (l, 0, j)),
            pl.BlockSpec((1, 1, n_block), lambda l, j: (l, 0, j)),
        ],
        out_specs=pl.BlockSpec((1, b, n_block), lambda l, j: (l, 0, j)),
        out_shape=jax.ShapeDtypeStruct((depth, b, n), F32),
        compiler_params=pltpu.CompilerParams(
            dimension_semantics=("arbitrary", "arbitrary"), vmem_limit_bytes=VMEM_LIMIT_BYTES),
        name="adaln",
    )(c, w_ada, b_ada.reshape(depth, 1, n))


def _cumsum_rows(tri, g):
    hi, lo = _split2(g)
    return _dot(tri, lo) + _dot(tri, hi)


def _decay_factors(b):
    half = CHUNK // 2
    ref = b[half:half + 1, :]
    last = b[CHUNK - 1:CHUNK, :]
    return jnp.exp(b - ref), jnp.exp(ref - b), jnp.exp(ref), jnp.exp(last - ref), jnp.exp(last)


def _silu(x):
    return x * jax.nn.sigmoid(x)


def _mixer_kernel(layer, ts, x_ref, mod_ref, g1_ref, win_ref, lbp_ref, wgk_ref, bgk_ref, gna_ref, gnb_ref,
                  wout_ref, xo_ref, p_ref, o_ref, sa_ref, sb_ref, qk_ref, kb_ref, qb_ref, v_ref, el_ref, sc_ref):
    d = x_ref.shape[-1]
    wa = d // 2
    wb = d - wa
    dka = wa // HGRN_HEADS
    kb_w = wb // 2
    dvb = wb // GLA_HEADS
    n_main = 4 * wa + 2 * kb_w + 2 * wb
    np_total = win_ref.shape[-1]
    n_chunks = ts // CHUNK
    c_qa, c_lf, c_ka, c_ga = 0, wa, 2 * wa, 3 * wa
    c_qb, c_kb = 4 * wa, 4 * wa + kb_w
    c_gb, c_la = 4 * wa + 2 * kb_w, 4 * wa + 2 * kb_w + wb

    @pl.when(pl.program_id(1) == 0)
    def _():
        sa_ref[...] = jnp.zeros_like(sa_ref)
        sb_ref[...] = jnp.zeros_like(sb_ref)

    sh1 = mod_ref[0, 0:1, :]
    sc1 = mod_ref[0, 1:2, :]
    gt1 = mod_ref[0, 2:3, :]

    x = x_ref[0]
    h = (_rms_rows(x) * g1_ref[...]) * (1.0 + sc1) + sh1
    hb = h.astype(BF16)

    def proj(c0, c1):
        return _dot(hb, win_ref[:, c0:c1])

    rows = [lbp_ref[i:i + 1, :] for i in range(lbp_ref.shape[0])]
    mx = functools.reduce(jnp.maximum, rows)
    ex = [jnp.exp(r - mx) for r in rows]
    tot = functools.reduce(lambda a, b: a + b, ex)
    sm = [e / tot for e in ex]
    cs = sm[0]
    for i in range(1, layer + 1):
        cs = cs + sm[i]
    lb = cs - sm[0]
    one_m_lb = 1.0 - lb

    ri = lax.broadcasted_iota(jnp.int32, (CHUNK, CHUNK), 0)
    ci = lax.broadcasted_iota(jnp.int32, (CHUNK, CHUNK), 1)
    causal = ri >= ci
    tri = jnp.where(causal, 1.0, 0.0).astype(BF16)
    lane = lax.broadcasted_iota(jnp.int32, (CHUNK, LANES), 1)
    half_masks = (lane < LANES // 2, lane >= LANES // 2)

    gna = gna_ref[...]
    gnb = gnb_ref[...]

    rb = proj(n_main, np_total).astype(BF16)
    p_ref[:, c_qa:c_qa + wa] = proj(0, wa)
    fa = proj(wa, 2 * wa)
    p_ref[:, c_lf:c_lf + wa] = jnp.log(lb + one_m_lb * jax.nn.sigmoid(fa))
    p_ref[:, c_ka:c_ka + wa] = one_m_lb * jax.nn.sigmoid(-fa)
    v_ref[:, 0:wa] = proj(2 * wa, 3 * wa).astype(BF16)
    p_ref[:, c_ga:c_ga + wa] = _silu(proj(3 * wa, 4 * wa))
    qk_b = proj(4 * wa, 4 * wa + 2 * kb_w)
    p_ref[:, c_qb:c_qb + kb_w] = qk_b[:, 0:kb_w] * (float(kb_w // GLA_HEADS) ** -0.5)
    p_ref[:, c_kb:c_kb + kb_w] = qk_b[:, kb_w:2 * kb_w]
    v_ref[:, wa:d] = proj(4 * wa + 2 * kb_w, 4 * wa + 2 * kb_w + wb).astype(BF16)
    p_ref[:, c_gb:c_gb + wb] = _silu(proj(4 * wa + 2 * kb_w + wb, n_main))
    gk = _dot(rb, wgk_ref[...]) + bgk_ref[...]
    p_ref[:, c_la:c_la + kb_w] = (jnp.minimum(gk, 0.0) - jnp.log1p(jnp.exp(-jnp.abs(gk)))) / GLA_GATE_NORM


    def stage_factors(c):
        rows_c = slice(c * CHUNK, (c + 1) * CHUNK)
        e_q, e_k, r_s, r_u, e_last = _decay_factors(_cumsum_rows(tri, p_ref[rows_c, c_lf:c_lf + wa]))
        qe = p_ref[rows_c, c_qa:c_qa + wa] * e_q
        ke = p_ref[rows_c, c_ka:c_ka + wa] * e_k
        qk_ref[0, rows_c, :] = qe.astype(BF16)
        qk_ref[1, rows_c, :] = ke.astype(BF16)
        qk_ref[2, rows_c, :] = (qe * r_s).astype(BF16)
        qk_ref[3, rows_c, :] = (ke * r_u).astype(BF16)
        el_ref[c:c + 1, 0:wa] = e_last
        e_q, e_k, r_s, r_u, e_last = _decay_factors(_cumsum_rows(tri, p_ref[rows_c, c_la:c_la + kb_w]))
        qe = p_ref[rows_c, c_qb:c_qb + kb_w] * e_q
        ke = p_ref[rows_c, c_kb:c_kb + kb_w] * e_k
        qb_ref[0, rows_c, :] = qe.astype(BF16)
        qb_ref[1, rows_c, :] = (qe * r_s).astype(BF16)
        ke_b = ke.astype(BF16)
        kd_b = (ke * r_u).astype(BF16)
        zero = jnp.zeros((CHUNK, LANES), BF16)
        for hd in range(GLA_HEADS):
            pair = slice((hd // 2) * LANES, (hd // 2 + 1) * LANES)
            msk = half_masks[hd % 2]
            kb_ref[0, rows_c, hd * LANES:(hd + 1) * LANES] = jnp.where(msk, ke_b[:, pair], zero)
            kb_ref[1, rows_c, hd * LANES:(hd + 1) * LANES] = jnp.where(msk, kd_b[:, pair], zero)
        el_ref[c:c + 1, wa:wa + kb_w] = e_last

    def stage_scores(c):
        rows_c = slice(c * CHUNK, (c + 1) * CHUNK)
        for hd in range(HGRN_HEADS):
            sl = slice(hd * dka, (hd + 1) * dka)
            scores = jnp.where(causal, _dot_nt(qk_ref[0, rows_c, sl], qk_ref[1, rows_c, sl]), 0.0)
            sc_ref[rows_c, hd * LANES:hd * LANES + CHUNK] = scores.astype(BF16)
        for hd in range(GLA_HEADS):
            pair = slice((hd // 2) * LANES, (hd // 2 + 1) * LANES)
            hs = slice(hd * LANES, (hd + 1) * LANES)
            scores = jnp.where(causal, _dot_nt(qb_ref[0, rows_c, pair], kb_ref[0, rows_c, hs]), 0.0)
            g = HGRN_HEADS + hd
            sc_ref[rows_c, g * LANES:g * LANES + CHUNK] = scores.astype(BF16)

    def stage_state(c):
        rows_c = slice(c * CHUNK, (c + 1) * CHUNK)

        def decay_cols(lanes):
            return jnp.broadcast_to(el_ref[c:c + 1, lanes], (LANES, LANES)).T

        for hd in range(HGRN_HEADS):
            sl = slice(hd * dka, (hd + 1) * dka)
            st = sa_ref[hd]
            sc = sc_ref[rows_c, hd * LANES:hd * LANES + CHUNK]
            o = _dot(sc, v_ref[rows_c, sl]) + _dot(qk_ref[2, rows_c, sl], st.astype(BF16))
            sa_ref[hd] = st * decay_cols(sl) + _dot_tn(qk_ref[3, rows_c, sl], v_ref[rows_c, sl])
            gate = p_ref[rows_c, c_ga + hd * dka:c_ga + (hd + 1) * dka]
            o_ref[rows_c, sl] = ((_rms_rows(o) * gna) * gate).astype(BF16)
        pair_decay = [decay_cols(slice(wa + pr * LANES, wa + (pr + 1) * LANES)) for pr in range(GLA_HEADS // 2)]
        for hd in range(GLA_HEADS):
            pair = slice((hd // 2) * LANES, (hd // 2 + 1) * LANES)
            hs = slice(hd * LANES, (hd + 1) * LANES)
            vs = slice(wa + hd * dvb, wa + (hd + 1) * dvb)
            st = sb_ref[hd]
            g = HGRN_HEADS + hd
            sc = sc_ref[rows_c, g * LANES:g * LANES + CHUNK]
            o = _dot(sc, v_ref[rows_c, vs]) + _dot(qb_ref[1, rows_c, pair], st.astype(BF16))
            sb_ref[hd] = st * pair_decay[hd // 2] + _dot_tn(kb_ref[1, rows_c, hs], v_ref[rows_c, vs])
            gate = p_ref[rows_c, c_gb + hd * dvb:c_gb + (hd + 1) * dvb]
            o_ref[rows_c, vs] = ((_rms_rows(o) * gnb) * gate).astype(BF16)

    for step in range(n_chunks + 2):
        if step >= 2:
            stage_state(step - 2)
        if 1 <= step <= n_chunks:
            stage_scores(step - 1)
        if step < n_chunks:
            stage_factors(step)

    ob = o_ref[...]
    col_block = 4 * LANES
    for c0 in range(0, d, col_block):
        c1 = c0 + col_block
        xo_ref[0, :, c0:c1] = x_ref[0, :, c0:c1] + gt1[:, c0:c1] * _dot(ob, wout_ref[:, c0:c1])


def _mixer(x, mod, g1, w_in_p, lb_params, w_gk_p, b_gk, gn_a, gn_b, w_out_b, layer, ts):
    bsz, seq, d = x.shape
    wa = d // 2
    kb_w = (d - wa) // 2
    const = lambda b, s: (0, 0)
    return pl.pallas_call(
        functools.partial(_mixer_kernel, layer, ts),
        grid=(bsz, seq // ts),
        in_specs=[
            pl.BlockSpec((1, ts, d), lambda b, s: (b, s, 0)),
            pl.BlockSpec((1, N_MOD, d), lambda b, s: (b, 0, 0)),
            pl.BlockSpec(g1.shape, const),
            pl.BlockSpec(w_in_p.shape, const),
            pl.BlockSpec(lb_params.shape, const),
            pl.BlockSpec(w_gk_p.shape, const),
            pl.BlockSpec(b_gk.shape, const),
            pl.BlockSpec(gn_a.shape, const),
            pl.BlockSpec(gn_b.shape, const),
            pl.BlockSpec(w_out_b.shape, const),
        ],
        out_specs=pl.BlockSpec((1, ts, d), lambda b, s: (b, s, 0)),
        out_shape=jax.ShapeDtypeStruct(x.shape, F32),
        scratch_shapes=[
            pltpu.VMEM((ts, 3 * d + kb_w), F32),
            pltpu.VMEM((ts, d), BF16),
            pltpu.VMEM((HGRN_HEADS, LANES, LANES), F32),
            pltpu.VMEM((GLA_HEADS, LANES, LANES), F32),
            pltpu.VMEM((4, ts, wa), BF16),
            pltpu.VMEM((2, ts, GLA_HEADS * LANES), BF16),
            pltpu.VMEM((2, ts, kb_w), BF16),
            pltpu.VMEM((ts, d), BF16),
            pltpu.VMEM((ts // CHUNK, wa + kb_w), F32),
            pltpu.VMEM((ts, (HGRN_HEADS + GLA_HEADS) * LANES), BF16),
        ],
        compiler_params=pltpu.CompilerParams(
            dimension_semantics=("arbitrary", "arbitrary"), vmem_limit_bytes=VMEM_LIMIT_BYTES),
        name="mixer",
    )(x, mod, g1, w_in_p, lb_params, w_gk_p, b_gk, gn_a, gn_b, w_out_b)


def _ffn_kernel(final, ts, fblk, x_ref, mod_ref, g2_ref, wup_ref, cw_ref, cb_ref, wdn_ref, gf_ref,
                xo_ref, hp_ref, hb_ref, ub_ref, halo_ref, act_ref, rp_ref):
    d = x_ref.shape[-1]
    f = wdn_ref.shape[0]
    seg = ts // SUBLANES
    front = (CONV_W - 1) * SUBLANES

    @pl.when(pl.program_id(1) == 0)
    def _():
        halo_ref[...] = jnp.zeros_like(halo_ref)

    sh2 = mod_ref[0, 3:4, :]
    sc2 = mod_ref[0, 4:5, :]
    gt2 = mod_ref[0, 5:6, :]

    h = (_rms_rows(x_ref[0]) * g2_ref[...]) * (1.0 + sc2) + sh2
    for s in range(d // LANES):
        for j in range(SUBLANES):
            hp_ref[s, pl.ds(j, seg, stride=SUBLANES), :] = h[j * seg:(j + 1) * seg, s * LANES:(s + 1) * LANES]
        hb_ref[:, s * LANES:(s + 1) * LANES] = hp_ref[s].astype(BF16)

    first_sublane = lax.broadcasted_iota(jnp.int32, (SUBLANES, fblk), 0) == 0

    def conv_block(plane, c0):
        cols = slice(c0, c0 + fblk)
        u = _dot(hb_ref[...], wup_ref[:, cols])
        for k in range(CONV_W - 1):
            cur = pltpu.roll(u[ts - front + k * SUBLANES:ts - front + (k + 1) * SUBLANES, :], 1, 0)
            prev = pltpu.roll(halo_ref[k * SUBLANES:(k + 1) * SUBLANES, cols], 1, 0)
            ub_ref[plane, k * SUBLANES:(k + 1) * SUBLANES, :] = jnp.where(first_sublane, prev, cur)
        ub_ref[plane, front:front + ts, :] = u
        halo_ref[:, cols] = u[ts - front:ts, :]
        y = cb_ref[:, cols]
        for k in range(CONV_W - 1):
            y = y + ub_ref[plane, k * SUBLANES:k * SUBLANES + ts, :] * cw_ref[k:k + 1, cols]
        return y + u * cw_ref[CONV_W - 1:CONV_W, cols]

    for j in range(f // fblk):
        a = conv_block(0, j * fblk)
        v = conv_block(1, f + j * fblk)
        act_ref[:, fblk * j:fblk * (j + 1)] = ((a * jax.nn.sigmoid(a)) * v).astype(BF16)

    act = act_ref[...]
    col_block = 4 * LANES
    for c0 in range(0, d, col_block):
        res = _dot(act, wdn_ref[:, c0:c0 + col_block])
        for s in range(c0 // LANES, (c0 + col_block) // LANES):
            rp_ref[s] = res[:, s * LANES - c0:(s + 1) * LANES - c0]
    for s in range(d // LANES):
        cs = slice(s * LANES, (s + 1) * LANES)
        for j in range(SUBLANES):
            rows = slice(j * seg, (j + 1) * seg)
            xo_ref[0, rows, cs] = x_ref[0, rows, cs] + gt2[:, cs] * rp_ref[s, pl.ds(j, seg, stride=SUBLANES), :]
    if final:
        xo_ref[0] = _rms_rows(xo_ref[0]) * gf_ref[...]


def _ffn(x, mod, g2, w_up_b, conv_w, conv_b, w_down_b, lnf_g, final, ts, fblk):
    bsz, seq, d = x.shape
    f2 = w_up_b.shape[-1]
    const = lambda b, s: (0, 0)
    return pl.pallas_call(
        functools.partial(_ffn_kernel, final, ts, fblk),
        grid=(bsz, seq // ts),
        in_specs=[
            pl.BlockSpec((1, ts, d), lambda b, s: (b, s, 0)),
            pl.BlockSpec((1, N_MOD, d), lambda b, s: (b, 0, 0)),
            pl.BlockSpec(g2.shape, const),
            pl.BlockSpec(w_up_b.shape, const),
            pl.BlockSpec(conv_w.shape, const),
            pl.BlockSpec(conv_b.shape, const),
            pl.BlockSpec(w_down_b.shape, const),
            pl.BlockSpec(lnf_g.shape, const),
        ],
        out_specs=pl.BlockSpec((1, ts, d), lambda b, s: (b, s, 0)),
        out_shape=jax.ShapeDtypeStruct(x.shape, F32),
        scratch_shapes=[
            pltpu.VMEM((d // LANES, ts, LANES), F32),
            pltpu.VMEM((ts, d), BF16),
            pltpu.VMEM((2, (CONV_W - 1) * SUBLANES + ts, fblk), F32),
            pltpu.VMEM(((CONV_W - 1) * SUBLANES, f2), F32),
            pltpu.VMEM((ts, f2 // 2), BF16),
            pltpu.VMEM((d // LANES, ts, LANES), F32),
        ],
        compiler_params=pltpu.CompilerParams(
            dimension_semantics=("arbitrary", "arbitrary"), vmem_limit_bytes=VMEM_LIMIT_BYTES),
        name="ffn_final" if final else "ffn",
    )(x, mod, g2, w_up_b, conv_w, conv_b, w_down_b, lnf_g)


def kernel(x, c, ln1_g, ln2_g, w_ada, b_ada, w_in, lb_params, w_gk, b_gk, gn_a, gn_b, w_out, w_up, conv_w,
           conv_b, w_down, lnf_g):
    bsz, seq, d = x.shape
    depth = w_ada.shape[0]
    ts = min(512, seq)
    fblk = 2 * LANES
    assert seq % ts == 0 and ts % CHUNK == 0 and d % (4 * LANES) == 0
    assert (w_up.shape[-1] // 2) % fblk == 0

    mod_all = _adaln(c, w_ada, b_ada, n_block=N_MOD * d // 4).reshape(depth, bsz, N_MOD, d)

    rank_pad = LANES - GLA_RANK
    w_in_p = jnp.pad(w_in, ((0, 0), (0, 0), (0, rank_pad))).astype(BF16)
    w_gk_p = jnp.pad(w_gk, ((0, 0), (0, rank_pad), (0, 0))).astype(BF16)
    w_out_b = w_out.astype(BF16)
    w_up_b = w_up.astype(BF16)
    w_down_b = w_down.astype(BF16)

    for l in range(depth):
        x = _mixer(x, mod_all[l], ln1_g[l][None, :], w_in_p[l], lb_params, w_gk_p[l], b_gk[l][None, :],
                   gn_a[l][None, :], gn_b[l][None, :], w_out_b[l], l, ts)
        x = _ffn(x, mod_all[l], ln2_g[l][None, :], w_up_b[l], conv_w[l], conv_b[l][None, :], w_down_b[l],
                 lnf_g[None, :], l == depth - 1, ts, fblk)
    return x
```

```python
import functools

import jax
import jax.numpy as jnp
from jax import lax
from jax.experimental import pallas as pl
from jax.experimental.pallas import tpu as pltpu

F32 = jnp.float32
BF16 = jnp.bfloat16

LANES = 128
SUBLANES = 8
VMEM_LIMIT_BYTES = 56 * 1024 * 1024
TILE_TOKENS = 1024

HGRN_HEADS = 4
GLA_HEADS = 4
GLA_RANK = 16
GLA_GATE_NORM = 16.0
CHUNK = 64
CONV_W = 3
N_MOD = 6
EPS = 1e-6

NT_DIMS = (((1,), (1,)), ((), ()))
TN_DIMS = (((0,), (0,)), ((), ()))


def _dot(a, b):
    return jnp.dot(a, b, preferred_element_type=F32)


def _dot_nt(a, b):
    return lax.dot_general(a, b, NT_DIMS, preferred_element_type=F32)


def _dot_tn(a, b):
    return lax.dot_general(a, b, TN_DIMS, preferred_element_type=F32)


def _split2(x):
    hi = x.astype(BF16)
    lo = (x - hi.astype(F32)).astype(BF16)
    return hi, lo


def _rms_rows(x):
    return x * lax.rsqrt(jnp.mean(x * x, axis=-1, keepdims=True) + EPS)


def _layer_spec(stacked, layer):
    return pl.BlockSpec((None,) + stacked.shape[1:], lambda b, s: (layer, 0, 0), pipeline_mode=pl.Buffered(1))


def _adaln_kernel(c_ref, w_ref, b_ref, o_ref):
    c = c_ref[...]
    cond = c * jax.nn.sigmoid(c)
    c_hi = cond.astype(BF16)
    c_lo = (cond - c_hi.astype(F32)).astype(BF16)
    w = w_ref[0]
    w_hi = w.astype(BF16)
    w_lo = (w - w_hi.astype(F32)).astype(BF16)
    acc = _dot(c_hi, w_lo) + _dot(c_lo, w_hi)
    o_ref[0] = (acc + _dot(c_hi, w_hi)) + b_ref[0]


def _adaln(c, w_ada, b_ada, n_block):
    depth, d, n = w_ada.shape
    b = c.shape[0]
    return pl.pallas_call(
        _adaln_kernel,
        grid=(depth, n // n_block),
        in_specs=[
            pl.BlockSpec((b, d), lambda l, j: (0, 0)),
            pl.BlockSpec((1, d, n_block), lambda l, j: (l, 0, j)),
            pl.BlockSpec((1, 1, n_block), lambda l, j: (l, 0, j)),
        ],
        out_specs=pl.BlockSpec((1, b, n_block), lambda l, j: (l, 0, j)),
        out_shape=jax.ShapeDtypeStruct((depth, b, n), F32),
        compiler_params=pltpu.CompilerParams(
            dimension_semantics=("arbitrary", "arbitrary"), vmem_limit_bytes=VMEM_LIMIT_BYTES),
        name="adaln",
    )(c, w_ada, b_ada.reshape(depth, 1, n))


def _cumsum_rows(tri, g):
    hi, lo = _split2(g)
    return _dot(tri, lo) + _dot(tri, hi)


def _decay_factors(b):
    half = CHUNK // 2
    ref = b[half:half + 1, :]
    last = b[CHUNK - 1:CHUNK, :]
    return jnp.exp(b - ref), jnp.exp(ref - b), jnp.exp(ref), jnp.exp(last - ref), jnp.exp(last)


def _silu(x):
    return x * jax.nn.sigmoid(x)


def _mixer_kernel(layer, ts, x_ref, mod_ref, g1_ref, win_ref, lbp_ref, wgk_ref, bgk_ref, gna_ref, gnb_ref,
                  wout_ref, xo_ref, p_ref, o_ref, sa_ref, sb_ref, qk_ref, kb_ref, qb_ref, v_ref, el_ref, sc_ref):
    d = x_ref.shape[-1]
    wa = d // 2
    wb = d - wa
    dka = wa // HGRN_HEADS
    kb_w = wb // 2
    dvb = wb // GLA_HEADS
    n_main = 4 * wa + 2 * kb_w + 2 * wb
    np_total = win_ref.shape[-1]
    n_chunks = ts // CHUNK
    c_qa, c_lf, c_ka, c_ga = 0, wa, 2 * wa, 3 * wa
    c_qb, c_kb = 4 * wa, 4 * wa + kb_w
    c_gb, c_la = 4 * wa + 2 * kb_w, 4 * wa + 2 * kb_w + wb

    @pl.when(pl.program_id(1) == 0)
    def _():
        sa_ref[...] = jnp.zeros_like(sa_ref)
        sb_ref[...] = jnp.zeros_like(sb_ref)

    sh1 = mod_ref[0, 0:1, :]
    sc1 = mod_ref[0, 1:2, :]
    gt1 = mod_ref[0, 2:3, :]

    x = x_ref[0]
    h = (_rms_rows(x) * g1_ref[...]) * (1.0 + sc1) + sh1
    hb = h.astype(BF16)

    def proj(c0, c1):
        return _dot(hb, win_ref[:, c0:c1])

    rows = [lbp_ref[i:i + 1, :] for i in range(lbp_ref.shape[0])]
    mx = functools.reduce(jnp.maximum, rows)
    ex = [jnp.exp(r - mx) for r in rows]
    tot = functools.reduce(lambda a, b: a + b, ex)
    sm = [e / tot for e in ex]
    cs = sm[0]
    for i in range(1, layer + 1):
        cs = cs + sm[i]
    lb = cs - sm[0]
    one_m_lb = 1.0 - lb

    ri = lax.broadcasted_iota(jnp.int32, (CHUNK, CHUNK), 0)
    ci = lax.broadcasted_iota(jnp.int32, (CHUNK, CHUNK), 1)
    causal = ri >= ci
    tri = jnp.where(causal, 1.0, 0.0).astype(BF16)
    lane = lax.broadcasted_iota(jnp.int32, (CHUNK, LANES), 1)
    half_masks = (lane < LANES // 2, lane >= LANES // 2)

    gna = gna_ref[...]
    gnb = gnb_ref[...]

    rb = proj(n_main, np_total).astype(BF16)
    p_ref[:, c_qa:c_qa + wa] = proj(0, wa)
    fa = proj(wa, 2 * wa)
    p_ref[:, c_lf:c_lf + wa] = jnp.log(lb + one_m_lb * jax.nn.sigmoid(fa))
    p_ref[:, c_ka:c_ka + wa] = one_m_lb * jax.nn.sigmoid(-fa)
    v_ref[:, 0:wa] = proj(2 * wa, 3 * wa).astype(BF16)
    p_ref[:, c_ga:c_ga + wa] = _silu(proj(3 * wa, 4 * wa))
    qk_b = proj(4 * wa, 4 * wa + 2 * kb_w)
    p_ref[:, c_qb:c_qb + kb_w] = qk_b[:, 0:kb_w] * (float(kb_w // GLA_HEADS) ** -0.5)
    p_ref[:, c_kb:c_kb + kb_w] = qk_b[:, kb_w:2 * kb_w]
    v_ref[:, wa:d] = proj(4 * wa + 2 * kb_w, 4 * wa + 2 * kb_w + wb).astype(BF16)
    p_ref[:, c_gb:c_gb + wb] = _silu(proj(4 * wa + 2 * kb_w + wb, n_main))
    gk = _dot(rb, wgk_ref[...]) + bgk_ref[...]
    p_ref[:, c_la:c_la + kb_w] = (jnp.minimum(gk, 0.0) - jnp.log1p(jnp.exp(-jnp.abs(gk)))) / GLA_GATE_NORM


    def stage_factors(c):
        rows_c = slice(c * CHUNK, (c + 1) * CHUNK)
        e_q, e_k, r_s, r_u, e_last = _decay_factors(_cumsum_rows(tri, p_ref[rows_c, c_lf:c_lf + wa]))
        qe = p_ref[rows_c, c_qa:c_qa + wa] * e_q
        ke = p_ref[rows_c, c_ka:c_ka + wa] * e_k
        qk_ref[0, rows_c, :] = qe.astype(BF16)
        qk_ref[1, rows_c, :] = ke.astype(BF16)
        qk_ref[2, rows_c, :] = (qe * r_s).astype(BF16)
        qk_ref[3, rows_c, :] = (ke * r_u).astype(BF16)
        el_ref[c:c + 1, 0:wa] = e_last
        e_q, e_k, r_s, r_u, e_last = _decay_factors(_cumsum_rows(tri, p_ref[rows_c, c_la:c_la + kb_w]))
        qe = p_ref[rows_c, c_qb:c_qb + kb_w] * e_q
        ke = p_ref[rows_c, c_kb:c_kb + kb_w] * e_k
        qb_ref[0, rows_c, :] = qe.astype(BF16)
        qb_ref[1, rows_c, :] = (qe * r_s).astype(BF16)
        ke_b = ke.astype(BF16)
        kd_b = (ke * r_u).astype(BF16)
        zero = jnp.zeros((CHUNK, LANES), BF16)
        for hd in range(GLA_HEADS):
            pair = slice((hd // 2) * LANES, (hd // 2 + 1) * LANES)
            msk = half_masks[hd % 2]
            kb_ref[0, rows_c, hd * LANES:(hd + 1) * LANES] = jnp.where(msk, ke_b[:, pair], zero)
            kb_ref[1, rows_c, hd * LANES:(hd + 1) * LANES] = jnp.where(msk, kd_b[:, pair], zero)
        el_ref[c:c + 1, wa:wa + kb_w] = e_last

    def stage_scores(c):
        rows_c = slice(c * CHUNK, (c + 1) * CHUNK)
        for hd in range(HGRN_HEADS):
            sl = slice(hd * dka, (hd + 1) * dka)
            scores = jnp.where(causal, _dot_nt(qk_ref[0, rows_c, sl], qk_ref[1, rows_c, sl]), 0.0)
            sc_ref[rows_c, hd * LANES:hd * LANES + CHUNK] = scores.astype(BF16)
        for hd in range(GLA_HEADS):
            pair = slice((hd // 2) * LANES, (hd // 2 + 1) * LANES)
            hs = slice(hd * LANES, (hd + 1) * LANES)
            scores = jnp.where(causal, _dot_nt(qb_ref[0, rows_c, pair], kb_ref[0, rows_c, hs]), 0.0)
            g = HGRN_HEADS + hd
            sc_ref[rows_c, g * LANES:g * LANES + CHUNK] = scores.astype(BF16)

    def stage_state(c):
        rows_c = slice(c * CHUNK, (c + 1) * CHUNK)

        def decay_cols(lanes):
            return jnp.broadcast_to(el_ref[c:c + 1, lanes], (LANES, LANES)).T

        for hd in range(HGRN_HEADS):
            sl = slice(hd * dka, (hd + 1) * dka)
            st = sa_ref[hd]
            sc = sc_ref[rows_c, hd * LANES:hd * LANES + CHUNK]
            o = _dot(sc, v_ref[rows_c, sl]) + _dot(qk_ref[2, rows_c, sl], st.astype(BF16))
            sa_ref[hd] = st * decay_cols(sl) + _dot_tn(qk_ref[3, rows_c, sl], v_ref[rows_c, sl])
            gate = p_ref[rows_c, c_ga + hd * dka:c_ga + (hd + 1) * dka]
            o_ref[rows_c, sl] = ((_rms_rows(o) * gna) * gate).astype(BF16)
        pair_decay = [decay_cols(slice(wa + pr * LANES, wa + (pr + 1) * LANES)) for pr in range(GLA_HEADS // 2)]
        for hd in range(GLA_HEADS):
            pair = slice((hd // 2) * LANES, (hd // 2 + 1) * LANES)
            hs = slice(hd * LANES, (hd + 1) * LANES)
            vs = slice(wa + hd * dvb, wa + (hd + 1) * dvb)
            st = sb_ref[hd]
            g = HGRN_HEADS + hd
            sc = sc_ref[rows_c, g * LANES:g * LANES + CHUNK]
            o = _dot(sc, v_ref[rows_c, vs]) + _dot(qb_ref[1, rows_c, pair], st.astype(BF16))
            sb_ref[hd] = st * pair_decay[hd // 2] + _dot_tn(kb_ref[1, rows_c, hs], v_ref[rows_c, vs])
            gate = p_ref[rows_c, c_gb + hd * dvb:c_gb + (hd + 1) * dvb]
            o_ref[rows_c, vs] = ((_rms_rows(o) * gnb) * gate).astype(BF16)

    for step in range(n_chunks + 2):
        if step >= 2:
            stage_state(step - 2)
        if 1 <= step <= n_chunks:
            stage_scores(step - 1)
        if step < n_chunks:
            stage_factors(step)

    ob = o_ref[...]
    col_block = 4 * LANES
    for c0 in range(0, d, col_block):
        c1 = c0 + col_block
        xo_ref[0, :, c0:c1] = x_ref[0, :, c0:c1] + gt1[:, c0:c1] * _dot(ob, wout_ref[:, c0:c1])


def _mixer(x, mod, g1, w_in_p, lb_params, w_gk_p, b_gk, gn_a, gn_b, w_out_b, layer, ts):
    bsz, seq, d = x.shape
    wa = d // 2
    kb_w = (d - wa) // 2
    return pl.pallas_call(
        functools.partial(_mixer_kernel, layer, ts),
        grid=(bsz, seq // ts),
        in_specs=[
            pl.BlockSpec((1, ts, d), lambda b, s: (b, s, 0)),
            pl.BlockSpec((None, 1, N_MOD, d), lambda b, s: (layer, b, 0, 0)),
            _layer_spec(g1, layer),
            _layer_spec(w_in_p, layer),
            pl.BlockSpec(lb_params.shape, lambda b, s: (0, 0)),
            _layer_spec(w_gk_p, layer),
            _layer_spec(b_gk, layer),
            _layer_spec(gn_a, layer),
            _layer_spec(gn_b, layer),
            _layer_spec(w_out_b, layer),
        ],
        out_specs=pl.BlockSpec((1, ts, d), lambda b, s: (b, s, 0)),
        out_shape=jax.ShapeDtypeStruct(x.shape, F32),
        scratch_shapes=[
            pltpu.VMEM((ts, 3 * d + kb_w), F32),
            pltpu.VMEM((ts, d), BF16),
            pltpu.VMEM((HGRN_HEADS, LANES, LANES), F32),
            pltpu.VMEM((GLA_HEADS, LANES, LANES), F32),
            pltpu.VMEM((4, ts, wa), BF16),
            pltpu.VMEM((2, ts, GLA_HEADS * LANES), BF16),
            pltpu.VMEM((2, ts, kb_w), BF16),
            pltpu.VMEM((ts, d), BF16),
            pltpu.VMEM((ts // CHUNK, wa + kb_w), F32),
            pltpu.VMEM((ts, (HGRN_HEADS + GLA_HEADS) * LANES), BF16),
        ],
        compiler_params=pltpu.CompilerParams(
            dimension_semantics=("arbitrary", "arbitrary"), vmem_limit_bytes=VMEM_LIMIT_BYTES),
        name="mixer",
    )(x, mod, g1, w_in_p, lb_params, w_gk_p, b_gk, gn_a, gn_b, w_out_b)


def _ffn_kernel(final, ts, fblk, x_ref, mod_ref, g2_ref, wup_ref, cw_ref, cb_ref, wdn_ref, gf_ref,
                xo_ref, hp_ref, hb_ref, ub_ref, halo_ref, act_ref, rp_ref):
    d = x_ref.shape[-1]
    f = wdn_ref.shape[0]
    seg = ts // SUBLANES
    front = (CONV_W - 1) * SUBLANES

    @pl.when(pl.program_id(1) == 0)
    def _():
        halo_ref[...] = jnp.zeros_like(halo_ref)

    sh2 = mod_ref[0, 3:4, :]
    sc2 = mod_ref[0, 4:5, :]
    gt2 = mod_ref[0, 5:6, :]

    h = (_rms_rows(x_ref[0]) * g2_ref[...]) * (1.0 + sc2) + sh2
    for s in range(d // LANES):
        for j in range(SUBLANES):
            hp_ref[s, pl.ds(j, seg, stride=SUBLANES), :] = h[j * seg:(j + 1) * seg, s * LANES:(s + 1) * LANES]
        hb_ref[:, s * LANES:(s + 1) * LANES] = hp_ref[s].astype(BF16)

    first_sublane = lax.broadcasted_iota(jnp.int32, (SUBLANES, fblk), 0) == 0

    def conv_block(plane, c0):
        cols = slice(c0, c0 + fblk)
        u = _dot(hb_ref[...], wup_ref[:, cols])
        for k in range(CONV_W - 1):
            cur = pltpu.roll(u[ts - front + k * SUBLANES:ts - front + (k + 1) * SUBLANES, :], 1, 0)
            prev = pltpu.roll(halo_ref[k * SUBLANES:(k + 1) * SUBLANES, cols], 1, 0)
            ub_ref[plane, k * SUBLANES:(k + 1) * SUBLANES, :] = jnp.where(first_sublane, prev, cur)
        ub_ref[plane, front:front + ts, :] = u
        halo_ref[:, cols] = u[ts - front:ts, :]
        y = cb_ref[:, cols]
        for k in range(CONV_W - 1):
            y = y + ub_ref[plane, k * SUBLANES:k * SUBLANES + ts, :] * cw_ref[k:k + 1, cols]
        return y + u * cw_ref[CONV_W - 1:CONV_W, cols]

    for j in range(f // fblk):
        a = conv_block(0, j * fblk)
        v = conv_block(1, f + j * fblk)
        act_ref[:, fblk * j:fblk * (j + 1)] = ((a * jax.nn.sigmoid(a)) * v).astype(BF16)

    act = act_ref[...]
    col_block = 4 * LANES
    for c0 in range(0, d, col_block):
        res = _dot(act, wdn_ref[:, c0:c0 + col_block])
        for s in range(c0 // LANES, (c0 + col_block) // LANES):
            rp_ref[s] = res[:, s * LANES - c0:(s + 1) * LANES - c0]
    for s in range(d // LANES):
        cs = slice(s * LANES, (s + 1) * LANES)
        for j in range(SUBLANES):
            rows = slice(j * seg, (j + 1) * seg)
            xo_ref[0, rows, cs] = x_ref[0, rows, cs] + gt2[:, cs] * rp_ref[s, pl.ds(j, seg, stride=SUBLANES), :]
    if final:
        xo_ref[0] = _rms_rows(xo_ref[0]) * gf_ref[...]


def _ffn(x, mod, g2, w_up_b, conv_w, conv_b, w_down_b, lnf_g, layer, final, ts, fblk):
    bsz, seq, d = x.shape
    f2 = w_up_b.shape[-1]
    return pl.pallas_call(
        functools.partial(_ffn_kernel, final, ts, fblk),
        grid=(bsz, seq // ts),
        in_specs=[
            pl.BlockSpec((1, ts, d), lambda b, s: (b, s, 0)),
            pl.BlockSpec((None, 1, N_MOD, d), lambda b, s: (layer, b, 0, 0)),
            _layer_spec(g2, layer),
            _layer_spec(w_up_b, layer),
            _layer_spec(conv_w, layer),
            _layer_spec(conv_b, layer),
            _layer_spec(w_down_b, layer),
            pl.BlockSpec(lnf_g.shape, lambda b, s: (0, 0)),
        ],
        out_specs=pl.BlockSpec((1, ts, d), lambda b, s: (b, s, 0)),
        out_shape=jax.ShapeDtypeStruct(x.shape, F32),
        scratch_shapes=[
            pltpu.VMEM((d // LANES, ts, LANES), F32),
            pltpu.VMEM((ts, d), BF16),
            pltpu.VMEM((2, (CONV_W - 1) * SUBLANES + ts, fblk), F32),
            pltpu.VMEM(((CONV_W - 1) * SUBLANES, f2), F32),
            pltpu.VMEM((ts, f2 // 2), BF16),
            pltpu.VMEM((d // LANES, ts, LANES), F32),
        ],
        compiler_params=pltpu.CompilerParams(
            dimension_semantics=("arbitrary", "arbitrary"), vmem_limit_bytes=VMEM_LIMIT_BYTES),
        name="ffn_final" if final else "ffn",
    )(x, mod, g2, w_up_b, conv_w, conv_b, w_down_b, lnf_g)


def kernel(x, c, ln1_g, ln2_g, w_ada, b_ada, w_in, lb_params, w_gk, b_gk, gn_a, gn_b, w_out, w_up, conv_w,
           conv_b, w_down, lnf_g):
    bsz, seq, d = x.shape
    depth = w_ada.shape[0]
    ts = min(TILE_TOKENS, seq)
    fblk = 2 * LANES
    assert seq % ts == 0 and ts % CHUNK == 0 and d % (4 * LANES) == 0
    assert (w_up.shape[-1] // 2) % fblk == 0

    mod_all = _adaln(c, w_ada, b_ada, n_block=N_MOD * d // 4).reshape(depth, bsz, N_MOD, d)

    rank_pad = LANES - GLA_RANK
    w_in_p = jnp.pad(w_in, ((0, 0), (0, 0), (0, rank_pad))).astype(BF16)
    w_gk_p = jnp.pad(w_gk, ((0, 0), (0, rank_pad), (0, 0))).astype(BF16)
    w_out_b = w_out.astype(BF16)
    w_up_b = w_up.astype(BF16)
    w_down_b = w_down.astype(BF16)

    row = lambda p: p[:, None, :]
    for l in range(depth):
        x = _mixer(x, mod_all, row(ln1_g), w_in_p, lb_params, w_gk_p, row(b_gk), row(gn_a), row(gn_b), w_out_b,
                   l, ts)
        x = _ffn(x, mod_all, row(ln2_g), w_up_b, conv_w, row(conv_b), w_down_b, lnf_g[None, :], l,
                 l == depth - 1, ts, fblk)
    return x
```

```python
import functools

import jax
import jax.numpy as jnp
from jax import lax
from jax.experimental import pallas as pl
from jax.experimental.pallas import tpu as pltpu

F32 = jnp.float32
BF16 = jnp.bfloat16

LANES = 128
SUBLANES = 8
VMEM_LIMIT_BYTES = 56 * 1024 * 1024
TILE_TOKENS = 1024

HGRN_HEADS = 4
GLA_HEADS = 4
GLA_RANK = 16
GLA_GATE_NORM = 16.0
CHUNK = 64
CONV_W = 3
N_MOD = 6
EPS = 1e-6

NT_DIMS = (((1,), (1,)), ((), ()))
TN_DIMS = (((0,), (0,)), ((), ()))


def _dot(a, b):
    return jnp.dot(a, b, preferred_element_type=F32)


def _dot_nt(a, b):
    return lax.dot_general(a, b, NT_DIMS, preferred_element_type=F32)


def _dot_tn(a, b):
    return lax.dot_general(a, b, TN_DIMS, preferred_element_type=F32)


def _split2(x):
    hi = x.astype(BF16)
    lo = (x - hi.astype(F32)).astype(BF16)
    return hi, lo


def _rms_rows(x):
    return x * lax.rsqrt(jnp.mean(x * x, axis=-1, keepdims=True) + EPS)


def _layer_spec(stacked, layer):
    return pl.BlockSpec((None,) + stacked.shape[1:], lambda b, s: (layer, 0, 0), pipeline_mode=pl.Buffered(1))


def _adaln_kernel(c_ref, w_ref, b_ref, o_ref):
    c = c_ref[...]
    cond = c * jax.nn.sigmoid(c)
    c_hi = cond.astype(BF16)
    c_lo = (cond - c_hi.astype(F32)).astype(BF16)
    w = w_ref[0]
    w_hi = w.astype(BF16)
    w_lo = (w - w_hi.astype(F32)).astype(BF16)
    acc = _dot(c_hi, w_lo) + _dot(c_lo, w_hi)
    o_ref[0] = (acc + _dot(c_hi, w_hi)) + b_ref[0]


def _adaln(c, w_ada, b_ada, n_block):
    depth, d, n = w_ada.shape
    b = c.shape[0]
    return pl.pallas_call(
        _adaln_kernel,
        grid=(depth, n // n_block),
        in_specs=[
            pl.BlockSpec((b, d), lambda l, j: (0, 0)),
            pl.BlockSpec((1, d, n_block), lambda l, j: (l, 0, j)),
            pl.BlockSpec((1, 1, n_block), lambda l, j: (l, 0, j)),
        ],
        out_specs=pl.BlockSpec((1, b, n_block), lambda l, j: (l, 0, j)),
        out_shape=jax.ShapeDtypeStruct((depth, b, n), F32),
        compiler_params=pltpu.CompilerParams(
            dimension_semantics=("arbitrary", "arbitrary"), vmem_limit_bytes=VMEM_LIMIT_BYTES),
        name="adaln",
    )(c, w_ada, b_ada.reshape(depth, 1, n))


def _cumsum_rows(tri, g):
    hi, lo = _split2(g)
    return _dot(tri, lo) + _dot(tri, hi)


def _decay_factors(b):
    half = CHUNK // 2
    ref = b[half:half + 1, :]
    last = b[CHUNK - 1:CHUNK, :]
    return jnp.exp(b - ref), jnp.exp(ref - b), jnp.exp(ref), jnp.exp(last - ref), jnp.exp(last)


def _silu(x):
    return x * jax.nn.sigmoid(x)


def _mixer_kernel(layer, ts, x_ref, mod_ref, g1_ref, win_ref, lbp_ref, wgk_ref, bgk_ref, gna_ref, gnb_ref,
                  wout_ref, xo_ref, p_ref, o_ref, sa_ref, sb_ref, qk_ref, kb_ref, qb_ref, v_ref, el_ref, sc_ref):
    d = x_ref.shape[-1]
    wa = d // 2
    wb = d - wa
    dka = wa // HGRN_HEADS
    kb_w = wb // 2
    dvb = wb // GLA_HEADS
    n_main = 4 * wa + 2 * kb_w + 2 * wb
    np_total = win_ref.shape[-1]
    n_chunks = ts // CHUNK
    c_qa, c_lf, c_ka, c_ga = 0, wa, 2 * wa, 3 * wa
    c_qb, c_kb = 4 * wa, 4 * wa + kb_w
    c_gb, c_la = 4 * wa + 2 * kb_w, 4 * wa + 2 * kb_w + wb

    @pl.when(pl.program_id(1) == 0)
    def _():
        sa_ref[...] = jnp.zeros_like(sa_ref)
        sb_ref[...] = jnp.zeros_like(sb_ref)

    sh1 = mod_ref[0, 0:1, :]
    sc1 = mod_ref[0, 1:2, :]
    gt1 = mod_ref[0, 2:3, :]

    x = x_ref[0]
    h = _rms_rows(x) * (g1_ref[...] * (1.0 + sc1)) + sh1
    hb = h.astype(BF16)

    def proj(c0, c1):
        return _dot(hb, win_ref[:, c0:c1])

    rows = [lbp_ref[i:i + 1, :] for i in range(lbp_ref.shape[0])]
    mx = functools.reduce(jnp.maximum, rows)
    ex = [jnp.exp(r - mx) for r in rows]
    tot = functools.reduce(lambda a, b: a + b, ex)
    sm = [e / tot for e in ex]
    cs = sm[0]
    for i in range(1, layer + 1):
        cs = cs + sm[i]
    lb = cs - sm[0]
    one_m_lb = 1.0 - lb

    ri = lax.broadcasted_iota(jnp.int32, (CHUNK, CHUNK), 0)
    ci = lax.broadcasted_iota(jnp.int32, (CHUNK, CHUNK), 1)
    causal = ri >= ci
    tri = jnp.where(causal, 1.0, 0.0).astype(BF16)
    lane = lax.broadcasted_iota(jnp.int32, (CHUNK, LANES), 1)
    half_masks = (lane < LANES // 2, lane >= LANES // 2)

    gna = gna_ref[...]
    gnb = gnb_ref[...]

    rb = proj(n_main, np_total).astype(BF16)
    p_ref[:, c_qa:c_qa + wa] = proj(0, wa)
    fa = proj(wa, 2 * wa)
    p_ref[:, c_lf:c_lf + wa] = jnp.log(lb + one_m_lb * jax.nn.sigmoid(fa))
    p_ref[:, c_ka:c_ka + wa] = one_m_lb * jax.nn.sigmoid(-fa)
    v_ref[:, 0:wa] = proj(2 * wa, 3 * wa).astype(BF16)
    p_ref[:, c_ga:c_ga + wa] = _silu(proj(3 * wa, 4 * wa))
    qk_b = proj(4 * wa, 4 * wa + 2 * kb_w)
    p_ref[:, c_qb:c_qb + kb_w] = qk_b[:, 0:kb_w] * (float(kb_w // GLA_HEADS) ** -0.5)
    p_ref[:, c_kb:c_kb + kb_w] = qk_b[:, kb_w:2 * kb_w]
    v_ref[:, wa:d] = proj(4 * wa + 2 * kb_w, 4 * wa + 2 * kb_w + wb).astype(BF16)
    p_ref[:, c_gb:c_gb + wb] = _silu(proj(4 * wa + 2 * kb_w + wb, n_main))
    gk = _dot(rb, wgk_ref[...]) + bgk_ref[...]
    p_ref[:, c_la:c_la + kb_w] = (jnp.minimum(gk, 0.0) - jnp.log1p(jnp.exp(-jnp.abs(gk)))) / GLA_GATE_NORM


    def stage_factors(c):
        rows_c = slice(c * CHUNK, (c + 1) * CHUNK)
        e_q, e_k, r_s, r_u, e_last = _decay_factors(_cumsum_rows(tri, p_ref[rows_c, c_lf:c_lf + wa]))
        qe = p_ref[rows_c, c_qa:c_qa + wa] * e_q
        ke = p_ref[rows_c, c_ka:c_ka + wa] * e_k
        qk_ref[0, rows_c, :] = qe.astype(BF16)
        qk_ref[1, rows_c, :] = ke.astype(BF16)
        qk_ref[2, rows_c, :] = (qe * r_s).astype(BF16)
        qk_ref[3, rows_c, :] = (ke * r_u).astype(BF16)
        el_ref[c:c + 1, 0:wa] = e_last
        e_q, e_k, r_s, r_u, e_last = _decay_factors(_cumsum_rows(tri, p_ref[rows_c, c_la:c_la + kb_w]))
        qe = p_ref[rows_c, c_qb:c_qb + kb_w] * e_q
        ke = p_ref[rows_c, c_kb:c_kb + kb_w] * e_k
        qb_ref[0, rows_c, :] = qe.astype(BF16)
        qb_ref[1, rows_c, :] = (qe * r_s).astype(BF16)
        ke_b = ke.astype(BF16)
        kd_b = (ke * r_u).astype(BF16)
        zero = jnp.zeros((CHUNK, LANES), BF16)
        for hd in range(GLA_HEADS):
            pair = slice((hd // 2) * LANES, (hd // 2 + 1) * LANES)
            msk = half_masks[hd % 2]
            kb_ref[0, rows_c, hd * LANES:(hd + 1) * LANES] = jnp.where(msk, ke_b[:, pair], zero)
            kb_ref[1, rows_c, hd * LANES:(hd + 1) * LANES] = jnp.where(msk, kd_b[:, pair], zero)
        el_ref[c:c + 1, wa:wa + kb_w] = e_last

    def stage_scores(c):
        rows_c = slice(c * CHUNK, (c + 1) * CHUNK)
        for hd in range(HGRN_HEADS):
            sl = slice(hd * dka, (hd + 1) * dka)
            scores = jnp.where(causal, _dot_nt(qk_ref[0, rows_c, sl], qk_ref[1, rows_c, sl]), 0.0)
            sc_ref[rows_c, hd * LANES:hd * LANES + CHUNK] = scores.astype(BF16)
        for hd in range(GLA_HEADS):
            pair = slice((hd // 2) * LANES, (hd // 2 + 1) * LANES)
            hs = slice(hd * LANES, (hd + 1) * LANES)
            scores = jnp.where(causal, _dot_nt(qb_ref[0, rows_c, pair], kb_ref[0, rows_c, hs]), 0.0)
            g = HGRN_HEADS + hd
            sc_ref[rows_c, g * LANES:g * LANES + CHUNK] = scores.astype(BF16)

    def stage_state(c):
        rows_c = slice(c * CHUNK, (c + 1) * CHUNK)

        def decay_cols(lanes):
            return jnp.broadcast_to(el_ref[c:c + 1, lanes], (LANES, LANES)).T

        for hd in range(HGRN_HEADS):
            sl = slice(hd * dka, (hd + 1) * dka)
            st = sa_ref[hd]
            sc = sc_ref[rows_c, hd * LANES:hd * LANES + CHUNK]
            o = _dot(sc, v_ref[rows_c, sl]) + _dot(qk_ref[2, rows_c, sl], st.astype(BF16))
            sa_ref[hd] = st * decay_cols(sl) + _dot_tn(qk_ref[3, rows_c, sl], v_ref[rows_c, sl])
            gate = p_ref[rows_c, c_ga + hd * dka:c_ga + (hd + 1) * dka]
            o_ref[rows_c, sl] = ((_rms_rows(o) * gna) * gate).astype(BF16)
        pair_decay = [decay_cols(slice(wa + pr * LANES, wa + (pr + 1) * LANES)) for pr in range(GLA_HEADS // 2)]
        for hd in range(GLA_HEADS):
            pair = slice((hd // 2) * LANES, (hd // 2 + 1) * LANES)
            hs = slice(hd * LANES, (hd + 1) * LANES)
            vs = slice(wa + hd * dvb, wa + (hd + 1) * dvb)
            st = sb_ref[hd]
            g = HGRN_HEADS + hd
            sc = sc_ref[rows_c, g * LANES:g * LANES + CHUNK]
            o = _dot(sc, v_ref[rows_c, vs]) + _dot(qb_ref[1, rows_c, pair], st.astype(BF16))
            sb_ref[hd] = st * pair_decay[hd // 2] + _dot_tn(kb_ref[1, rows_c, hs], v_ref[rows_c, vs])
            gate = p_ref[rows_c, c_gb + hd * dvb:c_gb + (hd + 1) * dvb]
            o_ref[rows_c, vs] = ((_rms_rows(o) * gnb) * gate).astype(BF16)

    for step in range(n_chunks + 2):
        if step >= 2:
            stage_state(step - 2)
        if 1 <= step <= n_chunks:
            stage_scores(step - 1)
        if step < n_chunks:
            stage_factors(step)

    ob = o_ref[...]
    col_block = 4 * LANES
    for c0 in range(0, d, col_block):
        c1 = c0 + col_block
        xo_ref[0, :, c0:c1] = x_ref[0, :, c0:c1] + gt1[:, c0:c1] * _dot(ob, wout_ref[:, c0:c1])


def _mixer(x, mod, g1, w_in_p, lb_params, w_gk_p, b_gk, gn_a, gn_b, w_out_b, layer, ts):
    bsz, seq, d = x.shape
    wa = d // 2
    kb_w = (d - wa) // 2
    return pl.pallas_call(
        functools.partial(_mixer_kernel, layer, ts),
        grid=(bsz, seq // ts),
        in_specs=[
            pl.BlockSpec((1, ts, d), lambda b, s: (b, s, 0)),
            pl.BlockSpec((None, 1, N_MOD, d), lambda b, s: (layer, b, 0, 0)),
            _layer_spec(g1, layer),
            _layer_spec(w_in_p, layer),
            pl.BlockSpec(lb_params.shape, lambda b, s: (0, 0)),
            _layer_spec(w_gk_p, layer),
            _layer_spec(b_gk, layer),
            _layer_spec(gn_a, layer),
            _layer_spec(gn_b, layer),
            _layer_spec(w_out_b, layer),
        ],
        out_specs=pl.BlockSpec((1, ts, d), lambda b, s: (b, s, 0)),
        out_shape=jax.ShapeDtypeStruct(x.shape, F32),
        scratch_shapes=[
            pltpu.VMEM((ts, 3 * d + kb_w), F32),
            pltpu.VMEM((ts, d), BF16),
            pltpu.VMEM((HGRN_HEADS, LANES, LANES), F32),
            pltpu.VMEM((GLA_HEADS, LANES, LANES), F32),
            pltpu.VMEM((4, ts, wa), BF16),
            pltpu.VMEM((2, ts, GLA_HEADS * LANES), BF16),
            pltpu.VMEM((2, ts, kb_w), BF16),
            pltpu.VMEM((ts, d), BF16),
            pltpu.VMEM((ts // CHUNK, wa + kb_w), F32),
            pltpu.VMEM((ts, (HGRN_HEADS + GLA_HEADS) * LANES), BF16),
        ],
        compiler_params=pltpu.CompilerParams(
            dimension_semantics=("arbitrary", "arbitrary"), vmem_limit_bytes=VMEM_LIMIT_BYTES),
        name="mixer",
    )(x, mod, g1, w_in_p, lb_params, w_gk_p, b_gk, gn_a, gn_b, w_out_b)


def _ffn_kernel(final, ts, fblk, x_ref, mod_ref, g2_ref, wup_ref, cw_ref, cb_ref, wdn_ref, gf_ref,
                xo_ref, hp_ref, hb_ref, ub_ref, halo_ref, act_ref, rp_ref):
    d = x_ref.shape[-1]
    f = wdn_ref.shape[0]
    seg = ts // SUBLANES
    front = (CONV_W - 1) * SUBLANES

    @pl.when(pl.program_id(1) == 0)
    def _():
        halo_ref[...] = jnp.zeros_like(halo_ref)

    sh2 = mod_ref[0, 3:4, :]
    sc2 = mod_ref[0, 4:5, :]
    gt2 = mod_ref[0, 5:6, :]

    h = _rms_rows(x_ref[0]) * (g2_ref[...] * (1.0 + sc2)) + sh2
    for s in range(d // LANES):
        for j in range(SUBLANES):
            hp_ref[s, pl.ds(j, seg, stride=SUBLANES), :] = h[j * seg:(j + 1) * seg, s * LANES:(s + 1) * LANES]
        hb_ref[:, s * LANES:(s + 1) * LANES] = hp_ref[s].astype(BF16)

    first_sublane = lax.broadcasted_iota(jnp.int32, (SUBLANES, fblk), 0) == 0

    def conv_block(plane, c0):
        cols = slice(c0, c0 + fblk)
        u = _dot(hb_ref[...], wup_ref[:, cols])
        for k in range(CONV_W - 1):
            cur = pltpu.roll(u[ts - front + k * SUBLANES:ts - front + (k + 1) * SUBLANES, :], 1, 0)
            prev = pltpu.roll(halo_ref[k * SUBLANES:(k + 1) * SUBLANES, cols], 1, 0)
            ub_ref[plane, k * SUBLANES:(k + 1) * SUBLANES, :] = jnp.where(first_sublane, prev, cur)
        ub_ref[plane, front:front + ts, :] = u
        halo_ref[:, cols] = u[ts - front:ts, :]
        y = cb_ref[:, cols]
        for k in range(CONV_W - 1):
            y = y + ub_ref[plane, k * SUBLANES:k * SUBLANES + ts, :] * cw_ref[k:k + 1, cols]
        return y + u * cw_ref[CONV_W - 1:CONV_W, cols]

    for j in range(f // fblk):
        a = conv_block(0, j * fblk)
        v = conv_block(1, f + j * fblk)
        act_ref[:, fblk * j:fblk * (j + 1)] = ((a * jax.nn.sigmoid(a)) * v).astype(BF16)

    act = act_ref[...]
    col_block = 4 * LANES
    for c0 in range(0, d, col_block):
        res = _dot(act, wdn_ref[:, c0:c0 + col_block])
        for s in range(c0 // LANES, (c0 + col_block) // LANES):
            rp_ref[s] = res[:, s * LANES - c0:(s + 1) * LANES - c0]
    for s in range(d // LANES):
        cs = slice(s * LANES, (s + 1) * LANES)
        for j in range(SUBLANES):
            rows = slice(j * seg, (j + 1) * seg)
            xo_ref[0, rows, cs] = x_ref[0, rows, cs] + gt2[:, cs] * rp_ref[s, pl.ds(j, seg, stride=SUBLANES), :]
    if final:
        xo_ref[0] = _rms_rows(xo_ref[0]) * gf_ref[...]


def _ffn(x, mod, g2, w_up_b, conv_w, conv_b, w_down_b, lnf_g, layer, final, ts, fblk):
    bsz, seq, d = x.shape
    f2 = w_up_b.shape[-1]
    return pl.pallas_call(
        functools.partial(_ffn_kernel, final, ts, fblk),
        grid=(bsz, seq // ts),
        in_specs=[
            pl.BlockSpec((1, ts, d), lambda b, s: (b, s, 0)),
            pl.BlockSpec((None, 1, N_MOD, d), lambda b, s: (layer, b, 0, 0)),
            _layer_spec(g2, layer),
            _layer_spec(w_up_b, layer),
            _layer_spec(conv_w, layer),
            _layer_spec(conv_b, layer),
            _layer_spec(w_down_b, layer),
            pl.BlockSpec(lnf_g.shape, lambda b, s: (0, 0)),
        ],
        out_specs=pl.BlockSpec((1, ts, d), lambda b, s: (b, s, 0)),
        out_shape=jax.ShapeDtypeStruct(x.shape, F32),
        scratch_shapes=[
            pltpu.VMEM((d // LANES, ts, LANES), F32),
            pltpu.VMEM((ts, d), BF16),
            pltpu.VMEM((2, (CONV_W - 1) * SUBLANES + ts, fblk), F32),
            pltpu.VMEM(((CONV_W - 1) * SUBLANES, f2), F32),
            pltpu.VMEM((ts, f2 // 2), BF16),
            pltpu.VMEM((d // LANES, ts, LANES), F32),
        ],
        compiler_params=pltpu.CompilerParams(
            dimension_semantics=("arbitrary", "arbitrary"), vmem_limit_bytes=VMEM_LIMIT_BYTES),
        name="ffn_final" if final else "ffn",
    )(x, mod, g2, w_up_b, conv_w, conv_b, w_down_b, lnf_g)


def kernel(x, c, ln1_g, ln2_g, w_ada, b_ada, w_in, lb_params, w_gk, b_gk, gn_a, gn_b, w_out, w_up, conv_w,
           conv_b, w_down, lnf_g):
    bsz, seq, d = x.shape
    depth = w_ada.shape[0]
    ts = min(TILE_TOKENS, seq)
    fblk = 2 * LANES
    assert seq % ts == 0 and ts % CHUNK == 0 and d % (4 * LANES) == 0
    assert (w_up.shape[-1] // 2) % fblk == 0

    mod_all = _adaln(c, w_ada, b_ada, n_block=N_MOD * d // 4).reshape(depth, bsz, N_MOD, d)

    rank_pad = LANES - GLA_RANK
    w_in_p = jnp.pad(w_in.astype(BF16), ((0, 0), (0, 0), (0, rank_pad)))
    w_gk_p = jnp.pad(w_gk.astype(BF16), ((0, 0), (0, rank_pad), (0, 0)))
    w_out_b = w_out.astype(BF16)
    w_up_b = w_up.astype(BF16)
    w_down_b = w_down.astype(BF16)

    row = lambda p: p[:, None, :]
    for l in range(depth):
        x = _mixer(x, mod_all, row(ln1_g), w_in_p, lb_params, w_gk_p, row(b_gk), row(gn_a), row(gn_b), w_out_b,
                   l, ts)
        x = _ffn(x, mod_all, row(ln2_g), w_up_b, conv_w, row(conv_b), w_down_b, lnf_g[None, :], l,
                 l == depth - 1, ts, fblk)
    return x
```
